```python
import math
import jax, jax.numpy as jnp
from jax import lax
import numpy as np

D_MODEL = 1024
BATCH = 4
SEQ = 8192
DEPTH = 2

N_EVEN = (DEPTH + 1) // 2
N_ODD = DEPTH // 2

GDN_HEAD_DIM = 128
GDN_WIDTH = D_MODEL // 2
GDN_HEADS = GDN_WIDTH // GDN_HEAD_DIM
CONV_K = 4
CHUNK = 64

S5_WIDTH = D_MODEL - GDN_WIDTH
S5_GROUP = 16
S5_GROUPS = S5_WIDTH // S5_GROUP
S5_STATE = 64

SB_HEAD_DIM = 128
SB_HEADS = D_MODEL // SB_HEAD_DIM
QBLOCK = 128

FFN_HIDDEN = -(-8 * D_MODEL // (3 * 256)) * 256

EVEN_IN = 4 * GDN_WIDTH + 2 * GDN_HEADS + S5_WIDTH
NORM_EPS = 1e-6

kernel_name = "hybrid_gdn_s5_stickbreak_adaln"


def rms_norm(x, w):
    xf = x.astype(jnp.float32)
    y = xf * lax.rsqrt(jnp.mean(xf * xf, axis=-1, keepdims=True) + NORM_EPS)
    return y.astype(x.dtype) * w


def l2_normalize(x):
    return x * lax.rsqrt(jnp.sum(x * x, axis=-1, keepdims=True) + NORM_EPS)


def causal_depthwise_conv(x, w):
    k_len, ch = w.shape
    return lax.conv_general_dilated(x, w[:, None, :].astype(x.dtype), window_strides=(1,),
                                    padding=[(k_len - 1, 0)],
                                    dimension_numbers=('NWC', 'WIO', 'NWC'),
                                    feature_group_count=ch)


def gated_delta_rule_chunked(q, k, v, g, beta):
    bsz, seq, nh, dk = q.shape
    dv = v.shape[-1]
    n = seq // CHUNK

    def to_chunks(t):
        return t.reshape(bsz, n, CHUNK, nh, -1).transpose(0, 3, 1, 2, 4)

    q, k, v = to_chunks(q), to_chunks(k), to_chunks(v)
    g = to_chunks(g[..., None])[..., 0]
    beta = to_chunks(beta[..., None])[..., 0]
    gc = jnp.cumsum(g, axis=-1)
    idx = jnp.arange(CHUNK)
    incl = idx[:, None] >= idx[None, :]
    strict = idx[:, None] > idx[None, :]
    decay = jnp.exp(jnp.where(incl, gc[..., :, None] - gc[..., None, :], -jnp.inf))

    k_beta = k * beta[..., None]
    lower = jnp.where(strict, jnp.einsum('bhnid,bhnjd->bhnij', k_beta, k) * decay, 0.0)
    lhs = lower + jnp.eye(CHUNK, dtype=lower.dtype)
    rhs = jnp.concatenate([v * beta[..., None], k_beta * jnp.exp(gc)[..., None]], axis=-1)
    sol = lax.linalg.triangular_solve(lhs, rhs, left_side=True, lower=True, unit_diagonal=True)
    u, w = sol[..., :dv], sol[..., dv:]

    attn = jnp.einsum('bhnid,bhnjd->bhnij', q, k) * decay
    q_dec = q * jnp.exp(gc)[..., None]
    g_last = gc[..., -1:]
    k_dec = k * jnp.exp(g_last - gc)[..., None]
    state_decay = jnp.exp(g_last[..., 0])

    xs = tuple(jnp.moveaxis(t, 2, 0) for t in (q_dec, k_dec, u, w, attn, state_decay))

    def step(state, inp):
        qd, kd, ui, wi, ai, sd = inp
        v_new = ui - jnp.einsum('bhck,bhkv->bhcv', wi, state)
        o = jnp.einsum('bhck,bhkv->bhcv', qd, state) + jnp.einsum('bhcs,bhsv->bhcv', ai, v_new)
        state = state * sd[..., None, None] + jnp.einsum('bhck,bhcv->bhkv', kd, v_new)
        return state, o

    s0 = jnp.zeros((bsz, nh, dk, dv), jnp.float32)
    _, o = lax.scan(step, s0, xs)
    return o.transpose(1, 0, 3, 2, 4).reshape(bsz, seq, nh, dv)


def s5_layer(u, lam_re, lam_im, log_dt, b_re, b_im, c_re, c_im, d_skip, glu_w, glu_b):
    f32 = jnp.float32
    bsz, seq, _ = u.shape
    uf = u.astype(f32).reshape(bsz, seq, S5_GROUPS, S5_GROUP)
    lr, li = lam_re.astype(f32), lam_im.astype(f32)
    dt = jnp.exp(log_dt.astype(f32))[:, None]
    mag = jnp.exp(lr * dt)
    lb_re, lb_im = mag * jnp.cos(li * dt), mag * jnp.sin(li * dt)
    den = lr * lr + li * li
    num_re, num_im = lb_re - 1.0, lb_im
    f_re = (num_re * lr + num_im * li) / den
    f_im = (num_im * lr - num_re * li) / den
    br, bi = b_re.astype(f32), b_im.astype(f32)
    bb_re = f_re[..., None] * br - f_im[..., None] * bi
    bb_im = f_re[..., None] * bi + f_im[..., None] * br
    bu_re = jnp.einsum('blgh,gph->lbgp', uf, bb_re)
    bu_im = jnp.einsum('blgh,gph->lbgp', uf, bb_im)
    a_re = jnp.broadcast_to(lb_re, (seq, 1, S5_GROUPS, S5_STATE))
    a_im = jnp.broadcast_to(lb_im, (seq, 1, S5_GROUPS, S5_STATE))

    def combine(e1, e2):
        a1r, a1i, b1r, b1i = e1
        a2r, a2i, b2r, b2i = e2
        return (a1r * a2r - a1i * a2i, a1r * a2i + a1i * a2r,
                a2r * b1r - a2i * b1i + b2r, a2r * b1i + a2i * b1r + b2i)

    _, _, xr, xi = lax.associative_scan(combine, (a_re, a_im, bu_re, bu_im), axis=0)
    y = (jnp.einsum('lbgp,ghp->blgh', xr, c_re.astype(f32))
         - jnp.einsum('lbgp,ghp->blgh', xi, c_im.astype(f32))
         + d_skip.astype(f32).reshape(S5_GROUPS, S5_GROUP) * uf)
    y = jax.nn.gelu(y.reshape(bsz, seq, S5_WIDTH))
    y = y * jax.nn.sigmoid(y @ glu_w.astype(f32) + glu_b.astype(f32))
    return y.astype(u.dtype)


def even_mixer(h, w_in, conv_w, a_log, dt_bias, head_norm_w, lam_re, lam_im, log_dt,
               b_re, b_im, c_re, c_im, d_skip, glu_w, glu_b, w_out):
    bsz, seq, _ = h.shape
    W, H = GDN_WIDTH, GDN_HEADS
    proj = h @ w_in
    qkv = jax.nn.silu(causal_depthwise_conv(proj[..., :3 * W], conv_w))
    z = proj[..., 3 * W:4 * W]
    b_lin = proj[..., 4 * W:4 * W + H]
    a_lin = proj[..., 4 * W + H:4 * W + 2 * H]
    u = proj[..., 4 * W + 2 * H:]

    f32 = jnp.float32
    q, k, v = (t.astype(f32).reshape(bsz, seq, H, GDN_HEAD_DIM) for t in jnp.split(qkv, 3, axis=-1))
    q = l2_normalize(q) * (GDN_HEAD_DIM ** -0.5)
    k = l2_normalize(k)
    beta = jax.nn.sigmoid(b_lin.astype(f32))
    g = -jnp.exp(a_log.astype(f32)) * jax.nn.softplus(a_lin.astype(f32) + dt_bias.astype(f32))
    o = gated_delta_rule_chunked(q, k, v, g, beta)
    o = rms_norm(o, head_norm_w.astype(f32)) * jax.nn.silu(z.astype(f32).reshape(bsz, seq, H, GDN_HEAD_DIM))
    o = o.reshape(bsz, seq, W).astype(h.dtype)

    y_s5 = s5_layer(u, lam_re, lam_im, log_dt, b_re, b_im, c_re, c_im, d_skip, glu_w, glu_b)
    return jnp.concatenate([o, y_s5], axis=-1) @ w_out


def stick_breaking_attention(q, k, v):
    bsz, seq, nh, d = q.shape
    nb = seq // QBLOCK
    f32 = jnp.float32
    scale = d ** -0.5
    qb = q.astype(f32).reshape(bsz, nb, QBLOCK, nh, d).transpose(1, 0, 3, 2, 4)
    kh = k.astype(f32).transpose(0, 2, 1, 3)
    vh = v.astype(f32).transpose(0, 2, 1, 3)
    key_pos = jnp.arange(seq)

    def block(args):
        qi, bi = args
        q_pos = bi * QBLOCK + jnp.arange(QBLOCK)
        z = jnp.einsum('bhqd,bhkd->bhqk', qi, kh) * scale
        valid = key_pos[None, :] < q_pos[:, None]
        log_1m = jnp.where(valid, jax.nn.log_sigmoid(-z), 0.0)
        between = lax.cumsum(log_1m, axis=3, reverse=True) - log_1m
        wts = jnp.where(valid, jnp.exp(jax.nn.log_sigmoid(z) + between), 0.0)
        return jnp.einsum('bhqk,bhkd->bhqd', wts, vh)

    o = lax.map(block, (qb, jnp.arange(nb)))
    return o.transpose(1, 0, 3, 2, 4).reshape(bsz, seq, nh, d).astype(q.dtype)


def odd_mixer(h, w_in, w_out):
    bsz, seq, _ = h.shape
    q, k, v = (t.reshape(bsz, seq, SB_HEADS, SB_HEAD_DIM) for t in jnp.split(h @ w_in, 3, axis=-1))
    o = stick_breaking_attention(q, k, v)
    return o.reshape(bsz, seq, D_MODEL) @ w_out


def swiglu(h, w_in, w_out):
    gate, up = jnp.split(h @ w_in, 2, axis=-1)
    return (jax.nn.silu(gate) * up) @ w_out


def setup_inputs(seed: int = 0) -> dict:
    key = jax.random.key(seed)
    ks = iter(jax.random.split(key, 32))
    f32 = jnp.float32
    D, F = D_MODEL, FFN_HIDDEN
    G, P, Hs = S5_GROUPS, S5_STATE, S5_GROUP

    def nrm(shape, scale):
        return scale * jax.random.normal(next(ks), shape, f32)

    def unif(shape, lo, hi):
        return jax.random.uniform(next(ks), shape, f32, lo, hi)

    x = nrm((BATCH, SEQ, D), 1.0)
    c = nrm((BATCH, D), 1.0)
    ada_w = nrm((DEPTH, D, 6 * D), D ** -0.5)
    ada_b = nrm((DEPTH, 6 * D), 0.02)
    norm_mix_w = 1.0 + nrm((DEPTH, D), 0.02)
    norm_ffn_w = 1.0 + nrm((DEPTH, D), 0.02)
    ffn_w_in = nrm((DEPTH, D, 2 * F), D ** -0.5)
    ffn_w_out = nrm((DEPTH, F, D), F ** -0.5)

    hy_w_in = nrm((N_EVEN, D, EVEN_IN), D ** -0.5)
    hy_conv_w = nrm((N_EVEN, CONV_K, 3 * GDN_WIDTH), CONV_K ** -0.5)
    hy_a_log = jnp.log(unif((N_EVEN, GDN_HEADS), 1.0, 16.0))
    dt = jnp.exp(unif((N_EVEN, GDN_HEADS), math.log(1e-3), math.log(1e-1)))
    hy_dt_bias = dt + jnp.log(-jnp.expm1(-dt))
    hy_head_norm_w = 1.0 + nrm((N_EVEN, GDN_HEAD_DIM), 0.02)
    s5_lam_re = -0.5 + nrm((N_EVEN, G, P), 0.01)
    s5_lam_im = jnp.pi * jnp.arange(P, dtype=f32) + nrm((N_EVEN, G, P), 0.01)
    s5_log_dt = unif((N_EVEN, G), math.log(1e-3), math.log(1e-1))
    s5_b_re = nrm((N_EVEN, G, P, Hs), (2 * Hs) ** -0.5)
    s5_b_im = nrm((N_EVEN, G, P, Hs), (2 * Hs) ** -0.5)
    s5_c_re = nrm((N_EVEN, G, Hs, P), 0.5)
    s5_c_im = nrm((N_EVEN, G, Hs, P), 0.5)
    s5_d = nrm((N_EVEN, S5_WIDTH), 1.0)
    s5_glu_w = nrm((N_EVEN, S5_WIDTH, S5_WIDTH), S5_WIDTH ** -0.5)
    s5_glu_b = nrm((N_EVEN, S5_WIDTH), 0.02)
    hy_w_out = nrm((N_EVEN, D, D), D ** -0.5)

    sb_w_in = nrm((N_ODD, D, 3 * D), D ** -0.5)
    sb_w_out = nrm((N_ODD, D, D), D ** -0.5)
    final_norm_w = 1.0 + nrm((D,), 0.02)
    return {"x": x, "c": c, "ada_w": ada_w, "ada_b": ada_b,
            "norm_mix_w": norm_mix_w, "norm_ffn_w": norm_ffn_w,
            "ffn_w_in": ffn_w_in, "ffn_w_out": ffn_w_out,
            "hy_w_in": hy_w_in, "hy_conv_w": hy_conv_w, "hy_a_log": hy_a_log,
            "hy_dt_bias": hy_dt_bias, "hy_head_norm_w": hy_head_norm_w,
            "s5_lam_re": s5_lam_re, "s5_lam_im": s5_lam_im, "s5_log_dt": s5_log_dt,
            "s5_b_re": s5_b_re, "s5_b_im": s5_b_im, "s5_c_re": s5_c_re, "s5_c_im": s5_c_im,
            "s5_d": s5_d, "s5_glu_w": s5_glu_w, "s5_glu_b": s5_glu_b, "hy_w_out": hy_w_out,
            "sb_w_in": sb_w_in, "sb_w_out": sb_w_out, "final_norm_w": final_norm_w}


def reference(x, c, ada_w, ada_b, norm_mix_w, norm_ffn_w, ffn_w_in, ffn_w_out,
              hy_w_in, hy_conv_w, hy_a_log, hy_dt_bias, hy_head_norm_w,
              s5_lam_re, s5_lam_im, s5_log_dt, s5_b_re, s5_b_im, s5_c_re, s5_c_im,
              s5_d, s5_glu_w, s5_glu_b, hy_w_out, sb_w_in, sb_w_out, final_norm_w):
    h = x
    c_act = jax.nn.silu(c)
    for i in range(DEPTH):
        mod = (c_act @ ada_w[i] + ada_b[i])[:, None, :]
        sh_m, sc_m, gt_m, sh_f, sc_f, gt_f = jnp.split(mod, 6, axis=-1)
        hn = rms_norm(h, norm_mix_w[i]) * (1.0 + sc_m) + sh_m
        j = i // 2
        if i % 2 == 0:
            y = even_mixer(hn, hy_w_in[j], hy_conv_w[j], hy_a_log[j], hy_dt_bias[j], hy_head_norm_w[j],
                           s5_lam_re[j], s5_lam_im[j], s5_log_dt[j], s5_b_re[j], s5_b_im[j],
                           s5_c_re[j], s5_c_im[j], s5_d[j], s5_glu_w[j], s5_glu_b[j], hy_w_out[j])
        else:
            y = odd_mixer(hn, sb_w_in[j], sb_w_out[j])
        h = h + gt_m * y
        hn = rms_norm(h, norm_ffn_w[i]) * (1.0 + sc_f) + sh_f
        h = h + gt_f * swiglu(hn, ffn_w_in[i], ffn_w_out[i])
    return rms_norm(h, final_norm_w)
```

```python
import functools

import jax
import jax.numpy as jnp
from jax import lax
from jax.experimental import pallas as pl
from jax.experimental.pallas import tpu as pltpu

F32 = jnp.float32
BF16 = jnp.bfloat16
NORM_EPS = 1e-6

GDN_HEAD_DIM = 128
CONV_K = 4
GDN_CHUNK = 64
S5_GROUP = 16
S5_STATE = 64
SB_HEAD_DIM = 128

LANES = 128
SUBLANES = 8
VMEM_LIMIT_BYTES = 56 * 1024 * 1024

ROW_TILE = 512
GDN_TILE = 128
S5_CHUNK = 16
SB_BLOCK = 256
MOD_COL_TILE = 1536


def _cparams(*sem):
    return pltpu.CompilerParams(dimension_semantics=sem, vmem_limit_bytes=VMEM_LIMIT_BYTES)


def _const_spec(shape):
    nd = len(shape)
    return pl.BlockSpec(shape, lambda *_: (0,) * nd, pipeline_mode=pl.Buffered(1))


def _dot(a, b):
    return jnp.dot(a, b, preferred_element_type=F32)


def _dot_nt(a, b):
    return lax.dot_general(a, b, (((1,), (1,)), ((), ())), preferred_element_type=F32)


def _dot_tn(a, b):
    return lax.dot_general(a, b, (((0,), (0,)), ((), ())), preferred_element_type=F32)


def _split2(x):
    hi = x.astype(BF16)
    lo = (x - hi.astype(F32)).astype(BF16)
    return hi, lo


def _mm3(a, b):
    ah, al = _split2(a)
    bh, bl = _split2(b)
    return _dot(ah, bh) + (_dot(ah, bl) + _dot(al, bh))


def _mm3_nt(a, b):
    ah, al = _split2(a)
    bh, bl = _split2(b)
    return _dot_nt(ah, bh) + (_dot_nt(ah, bl) + _dot_nt(al, bh))


def _sigmoid(x):
    return 1.0 / (1.0 + jnp.exp(-x))


def _silu(x):
    return x * _sigmoid(x)


def _softplus(x):
    return jnp.maximum(x, 0.0) + jnp.log(1.0 + jnp.exp(-jnp.abs(x)))


def _rms(x):
    return x * lax.rsqrt(jnp.mean(x * x, axis=-1, keepdims=True) + NORM_EPS)


def _norm_mod(x, nw, sc, sh):
    return (_rms(x) * nw) * (1.0 + sc) + sh


def _mod_kernel(c_ref, w_ref, b_ref, o_ref):
    ca = _silu(c_ref[...])
    o_ref[0] = _mm3(ca, w_ref[0]) + b_ref[0]


def _modulation(c, ada_w, ada_b):
    depth, d, n = ada_w.shape
    bsz = c.shape[0]
    rows = -(-bsz // SUBLANES) * SUBLANES
    c_pad = jnp.pad(c, ((0, rows - bsz), (0, 0)))
    tn = MOD_COL_TILE
    return pl.pallas_call(
        _mod_kernel,
        grid=(depth, n // tn),
        in_specs=[pl.BlockSpec((rows, d), lambda i, j: (0, 0)),
                  pl.BlockSpec((1, d, tn), lambda i, j: (i, 0, j)),
                  pl.BlockSpec((1, 1, tn), lambda i, j: (i, 0, j))],
        out_specs=pl.BlockSpec((1, rows, tn), lambda i, j: (i, 0, j)),
        out_shape=jax.ShapeDtypeStruct((depth, rows, n), F32),
        compiler_params=_cparams("arbitrary", "arbitrary"),
        name="adaln_modulation",
    )(c_pad, ada_w, ada_b.reshape(depth, 1, n))


def _inproj_kernel(h_ref, nw_ref, sc_ref, sh_ref, w_ref, *out_refs, widths):
    hn = _norm_mod(h_ref[0], nw_ref[...], sc_ref[0], sh_ref[0]).astype(BF16)
    off = 0
    for o_ref, wd in zip(out_refs, widths):
        o_ref[0] = _dot(hn, w_ref[:, off:off + wd]).astype(o_ref.dtype)
        off += wd


def _inproj(h, nw, sc, sh, w, widths, dtypes):
    bsz, seq, d = h.shape
    tm = min(ROW_TILE, seq)
    n = w.shape[1]
    row = lambda b, t: (b, t, 0)
    per_b = lambda b, t: (b, 0, 0)
    return pl.pallas_call(
        functools.partial(_inproj_kernel, widths=widths),
        grid=(bsz, seq // tm),
        in_specs=[pl.BlockSpec((1, tm, d), row),
                  pl.BlockSpec((1, d), lambda b, t: (0, 0)),
                  pl.BlockSpec((1, 1, d), per_b),
                  pl.BlockSpec((1, 1, d), per_b),
                  _const_spec((d, n))],
        out_specs=[pl.BlockSpec((1, tm, wd), row) for wd in widths],
        out_shape=[jax.ShapeDtypeStruct((bsz, seq, wd), dt) for wd, dt in zip(widths, dtypes)],
        compiler_params=_cparams("arbitrary", "arbitrary"),
        name="norm_mod_inproj",
    )(h, nw.reshape(1, d), sc, sh, w)


def _unit_lower_inverse(a, ri, ci):
    n = a.shape[0]
    eye = (ri == ci).astype(F32)
    t = eye - jnp.where((ri // 2 == ci // 2) & (ri > ci), a, 0.0)
    s = 2
    while s < n:
        sel = (ri // (2 * s) == ci // (2 * s)) & (ri // s > ci // s)
        al = jnp.where(sel, a, 0.0)
        t = t - _mm3(_mm3(t, al), t)
        s *= 2
    return t


def _gdn_kernel(x_ref, z_ref, ba_ref, cw_ref, garow_ref, gbrow_ref, hnw_ref, o_ref,
                xbuf_ref, state_ref, *, heads, tile):
    dh = GDN_HEAD_DIM
    width = heads * dh
    ck = GDN_CHUNK
    nck = tile // ck
    t_id = pl.program_id(1)

    @pl.when(t_id == 0)
    def _():
        xbuf_ref[0:SUBLANES, :] = jnp.zeros((SUBLANES, 3 * width), F32)
        state_ref[...] = jnp.zeros(state_ref.shape, F32)

    x = x_ref[0]
    xbuf_ref[SUBLANES:SUBLANES + tile, :] = x
    conv = cw_ref[0:1, :] * xbuf_ref[pl.ds(SUBLANES - CONV_K + 1, tile), :]
    for k in range(1, CONV_K):
        conv = conv + cw_ref[k:k + 1, :] * xbuf_ref[pl.ds(SUBLANES - CONV_K + 1 + k, tile), :]
    xbuf_ref[0:SUBLANES, :] = x[tile - SUBLANES:tile, :]
    qkv = _silu(conv)

    ba = ba_ref[0]
    beta_all = _sigmoid(ba)
    g_all = garow_ref[...] * _softplus(ba + gbrow_ref[...])
    ri_t = lax.broadcasted_iota(jnp.int32, (tile, tile), 0)
    ci_t = lax.broadcasted_iota(jnp.int32, (tile, tile), 1)
    cum_mat = ((ri_t // ck == ci_t // ck) & (ri_t >= ci_t)).astype(BF16)
    g_hi = g_all.astype(BF16)
    g_r1 = g_all - g_hi.astype(F32)
    g_mid = g_r1.astype(BF16)
    g_lo = (g_r1 - g_mid.astype(F32)).astype(BF16)
    gc_all = _dot(cum_mat, g_hi) + (_dot(cum_mat, g_mid) + _dot(cum_mat, g_lo))
    gc_all_t = gc_all.T

    ri = lax.broadcasted_iota(jnp.int32, (ck, ck), 0)
    ci = lax.broadcasted_iota(jnp.int32, (ck, ck), 1)
    hnw = hnw_ref[...]
    z_all = z_ref[0]

    for h in range(heads):
        q_h = qkv[:, h * dh:(h + 1) * dh]
        k_h = qkv[:, width + h * dh:width + (h + 1) * dh]
        v_h = qkv[:, 2 * width + h * dh:2 * width + (h + 1) * dh]
        q_h = q_h * lax.rsqrt(jnp.sum(q_h * q_h, axis=-1, keepdims=True) + NORM_EPS) * (dh ** -0.5)
        k_h = k_h * lax.rsqrt(jnp.sum(k_h * k_h, axis=-1, keepdims=True) + NORM_EPS)
        beta_h = beta_all[:, h:h + 1]
        gc_col_h = gc_all[:, heads + h:heads + h + 1]
        gc_row_h = gc_all_t[heads + h:heads + h + 1, :]
        state = state_ref[h]
        outs = []
        for n in range(nck):
            sl = slice(n * ck, (n + 1) * ck)
            q, k, v = q_h[sl], k_h[sl], v_h[sl]
            beta = beta_h[sl]
            gcc = gc_col_h[sl]
            gcr = gc_row_h[:, sl]
            g_last = gcc[ck - 1:ck, :]
            decay = jnp.where(ri >= ci, jnp.exp(jnp.minimum(gcc - gcr, 0.0)), 0.0)
            k_beta = k * beta
            kb16, k16 = k_beta.astype(BF16), k.astype(BF16)
            a = jnp.where(ri > ci, _dot_nt(kb16, k16) * decay, 0.0)
            egc = jnp.exp(gcc)
            rhs = jnp.concatenate([v * beta, k_beta * egc], axis=-1)
            sol = _mm3(_unit_lower_inverse(a, ri, ci), rhs)
            u, w = sol[:, :dh], sol[:, dh:]
            attn = _dot_nt(q.astype(BF16), k16) * decay
            q_dec = q * egc
            k_dec = k * jnp.exp(g_last - gcc)
            s16 = state.astype(BF16)
            v_new = u - _dot(w.astype(BF16), s16)
            vn16 = v_new.astype(BF16)
            outs.append(_dot(q_dec.astype(BF16), s16) + _dot(attn.astype(BF16), vn16))
            state = state * jnp.exp(g_last) + _dot_tn(k_dec.astype(BF16), vn16)
        state_ref[h] = state
        o_h = jnp.concatenate(outs, axis=0) if nck > 1 else outs[0]
        o_h = (_rms(o_h) * hnw) * _silu(z_all[:, h * dh:(h + 1) * dh])
        o_ref[0, :, h * dh:(h + 1) * dh] = o_h.astype(o_ref.dtype)


def _gdn(qkv_pre, z, ba, conv_w, a_log, dt_bias, head_norm_w):
    bsz, seq, w3 = qkv_pre.shape
    width = w3 // 3
    heads = width // GDN_HEAD_DIM
    tile = min(GDN_TILE, seq)
    garow = jnp.zeros((1, LANES), F32).at[0, heads:2 * heads].set(-jnp.exp(a_log.astype(F32)))
    gbrow = jnp.zeros((1, LANES), F32).at[0, heads:2 * heads].set(dt_bias.astype(F32))
    row = lambda b, t: (b, t, 0)
    fix2 = lambda b, t: (0, 0)
    return pl.pallas_call(
        functools.partial(_gdn_kernel, heads=heads, tile=tile),
        grid=(bsz, seq // tile),
        in_specs=[pl.BlockSpec((1, tile, w3), row),
                  pl.BlockSpec((1, tile, width), row),
                  pl.BlockSpec((1, tile, LANES), row),
                  pl.BlockSpec((CONV_K, w3), fix2),
                  pl.BlockSpec((1, LANES), fix2),
                  pl.BlockSpec((1, LANES), fix2),
                  pl.BlockSpec((1, GDN_HEAD_DIM), fix2)],
        out_specs=pl.BlockSpec((1, tile, width), row),
        out_shape=jax.ShapeDtypeStruct((bsz, seq, width), BF16),
        scratch_shapes=[pltpu.VMEM((SUBLANES + tile, w3), F32),
                        pltpu.VMEM((heads, GDN_HEAD_DIM, GDN_HEAD_DIM), F32)],
        compiler_params=_cparams("arbitrary", "arbitrary"),
        name="gated_deltanet",
    )(qkv_pre, z, ba, conv_w.astype(F32), garow, gbrow, head_norm_w.astype(F32).reshape(1, -1))


def _s5_prep_kernel(lr_ref, li_ref, ldt_ref, btr_ref, bti_ref, cr_ref, ci_ref,
                    kst_ref, bjr_ref, bji_ref, cor_ref, coi_ref, l16r_ref, l16i_ref):
    nt = S5_CHUNK
    lr, li = lr_ref[0], li_ref[0]
    dt = jnp.exp(ldt_ref[0])
    mag = jnp.exp(lr * dt)
    lb_re, lb_im = mag * jnp.cos(li * dt), mag * jnp.sin(li * dt)
    den = lr * lr + li * li
    num_re, num_im = lb_re - 1.0, lb_im
    f_re = (num_re * lr + num_im * li) / den
    f_im = (num_im * lr - num_re * li) / den
    btr, bti = btr_ref[0], bti_ref[0]
    bb_re = f_re * btr - f_im * bti
    bb_im = f_re * bti + f_im * btr
    cr, ci = cr_ref[0], ci_ref[0]
    p_re, p_im = [jnp.ones_like(lr)], [jnp.zeros_like(lr)]
    for _ in range(nt):
        p_re.append(p_re[-1] * lb_re - p_im[-1] * lb_im)
        p_im.append(p_re[-2] * lb_im + p_im[-1] * lb_re)
    cp_re = [cr * pr - ci * pi for pr, pi in zip(p_re, p_im)]
    cp_im = [cr * pi + ci * pr for pr, pi in zip(p_re, p_im)]
    kst_ref[0] = (_mm3_nt(jnp.concatenate(cp_re[:nt], axis=0), bb_re)
                  - _mm3_nt(jnp.concatenate(cp_im[:nt], axis=0), bb_im))
    bjr_ref[0] = jnp.concatenate([bb_re * p_re[nt - 1 - s] - bb_im * p_im[nt - 1 - s] for s in range(nt)], axis=0)
    bji_ref[0] = jnp.concatenate([bb_re * p_im[nt - 1 - s] + bb_im * p_re[nt - 1 - s] for s in range(nt)], axis=0)
    cor_ref[0] = jnp.concatenate(cp_re[1:nt + 1], axis=0)
    coi_ref[0] = jnp.concatenate(cp_im[1:nt + 1], axis=0)
    l16r_ref[0] = p_re[nt]
    l16i_ref[0] = p_im[nt]


def _s5_prep(lam_re, lam_im, log_dt, b_re, b_im, c_re, c_im):
    g, p = lam_re.shape
    hs = c_re.shape[1]
    nt = S5_CHUNK
    vec = pl.BlockSpec((1, 1, p), lambda i: (i, 0, 0))
    mat = pl.BlockSpec((1, hs, p), lambda i: (i, 0, 0))
    big = pl.BlockSpec((1, nt * hs, p), lambda i: (i, 0, 0))
    f32 = lambda a: a.astype(F32)
    return pl.pallas_call(
        _s5_prep_kernel,
        grid=(g,),
        in_specs=[vec, vec, pl.BlockSpec((1, 1, 1), lambda i: (i, 0, 0)), mat, mat, mat, mat],
        out_specs=[pl.BlockSpec((1, nt * hs, hs), lambda i: (i, 0, 0)), big, big, big, big, vec, vec],
        out_shape=[jax.ShapeDtypeStruct((g, nt * hs, hs), F32)]
                  + [jax.ShapeDtypeStruct((g, nt * hs, p), F32)] * 4
                  + [jax.ShapeDtypeStruct((g, 1, p), F32)] * 2,
        compiler_params=_cparams("arbitrary"),
        name="s5_discretise",
    )(f32(lam_re).reshape(g, 1, p), f32(lam_im).reshape(g, 1, p), f32(log_dt).reshape(g, 1, 1),
      f32(b_re).transpose(0, 2, 1), f32(b_im).transpose(0, 2, 1), f32(c_re), f32(c_im))


def _s5_kernel(u_ref, toep_ref, binj_ref, ct_ref, dt_ref, l16r_ref, l16i_ref, y_ref,
               sre_ref, sim_ref, xre_ref, xim_ref, *, streams):
    rows = u_ref.shape[1]
    steps = rows // streams
    u = u_ref[0]
    ub = u.astype(BF16)
    inj = _dot(ub, binj_ref[0])
    sre_ref[...] = inj[:, :LANES]
    sim_ref[...] = inj[:, LANES:]
    lr = jnp.broadcast_to(l16r_ref[0], (streams, LANES))
    li = jnp.broadcast_to(l16i_ref[0], (streams, LANES))

    def scan_step(i, carry):
        xr, xi = carry
        r0 = pl.multiple_of(i * streams, streams)
        xre_ref[pl.ds(r0, streams), :] = xr
        xim_ref[pl.ds(r0, streams), :] = xi
        sr = sre_ref[pl.ds(r0, streams), :]
        si = sim_ref[pl.ds(r0, streams), :]
        return lr * xr - li * xi + sr, lr * xi + li * xr + si

    zero = jnp.zeros((streams, LANES), F32)
    xr_end, xi_end = lax.fori_loop(0, steps, scan_step, (zero, zero), unroll=8)

    odd = (lax.broadcasted_iota(jnp.int32, (streams, LANES), 0) % 2) == 1
    zr0 = jnp.where(odd, pltpu.roll(xr_end, 1, axis=0), 0.0)
    zi0 = jnp.where(odd, pltpu.roll(xi_end, 1, axis=0), 0.0)

    def carry_step(i, carry):
        zr, zi = carry
        r0 = pl.multiple_of(i * streams, streams)
        xre_ref[pl.ds(r0, streams), :] = xre_ref[pl.ds(r0, streams), :] + zr
        xim_ref[pl.ds(r0, streams), :] = xim_ref[pl.ds(r0, streams), :] + zi
        return lr * zr - li * zi, lr * zi + li * zr

    lax.fori_loop(0, steps, carry_step, (zr0, zi0), unroll=8)

    xin = jnp.concatenate([xre_ref[...], xim_ref[...]], axis=1).astype(BF16)
    y_ref[0] = _dot(ub, toep_ref[0]) + _dot(xin, ct_ref[0]) + dt_ref[0] * u


def _s5(u, lam_re, lam_im, log_dt, b_re, b_im, c_re, c_im, d_skip):
    bsz, seq, width = u.shape
    hs, p, nt = S5_GROUP, S5_STATE, S5_CHUNK
    g = width // hs
    streams = 2 * bsz
    assert streams == SUBLANES and p * 2 == LANES and nt * hs == 2 * LANES
    steps = seq // (2 * nt)
    rows = steps * streams
    kst, bjr, bji, cor, coi, l16r, l16i = _s5_prep(lam_re, lam_im, log_dt, b_re, b_im, c_re, c_im)

    s_idx = jnp.arange(nt)
    tau = s_idx[None, :] - s_idx[:, None]
    k4 = kst.reshape(g, nt, hs, hs)
    toep = jnp.where((tau >= 0)[None, :, :, None, None], k4[:, jnp.clip(tau, 0, nt - 1)], 0.0)
    toep = toep.transpose(0, 1, 4, 2, 3).reshape(g, nt * hs, nt * hs).astype(BF16)
    zpad = jnp.zeros((g, nt * hs, LANES - p), F32)
    binj = jnp.concatenate([bjr, zpad, bji, zpad], axis=-1).astype(BF16)
    zrow = jnp.zeros((g, LANES - p, nt * hs), F32)
    ct = jnp.concatenate([cor.transpose(0, 2, 1), zrow, -coi.transpose(0, 2, 1), zrow], axis=1).astype(BF16)
    dtile = jnp.tile(d_skip.astype(F32).reshape(g, 1, hs), (1, 1, nt))
    lpad = jnp.zeros((g, 1, LANES - p), F32)
    l16r = jnp.concatenate([l16r, lpad], axis=-1)
    l16i = jnp.concatenate([l16i, lpad], axis=-1)

    ut = u.reshape(bsz, 2, steps, nt, g, hs).transpose(4, 2, 0, 1, 3, 5).reshape(g, rows, nt * hs)
    per_g3 = lambda i: (i, 0, 0)
    y = pl.pallas_call(
        functools.partial(_s5_kernel, streams=streams),
        grid=(g,),
        in_specs=[pl.BlockSpec((1, rows, nt * hs), per_g3),
                  pl.BlockSpec((1, nt * hs, nt * hs), per_g3),
                  pl.BlockSpec((1, nt * hs, 2 * LANES), per_g3),
                  pl.BlockSpec((1, 2 * LANES, nt * hs), per_g3),
                  pl.BlockSpec((1, 1, nt * hs), per_g3),
                  pl.BlockSpec((1, 1, LANES), per_g3),
                  pl.BlockSpec((1, 1, LANES), per_g3)],
        out_specs=pl.BlockSpec((1, rows, nt * hs), per_g3),
        out_shape=jax.ShapeDtypeStruct((g, rows, nt * hs), F32),
        scratch_shapes=[pltpu.VMEM((rows, LANES), F32)] * 4,
        compiler_params=_cparams("arbitrary"),
        name="s5_chunked_scan",
    )(ut, toep, binj, ct, dtile, l16r, l16i)
    return y.reshape(g, steps, bsz, 2, nt, hs).transpose(2, 3, 1, 4, 0, 5).reshape(bsz, seq, width)


def _sb_kernel(q_ref, k_ref, v_ref, o_ref, acc_ref, run_ref, *, blk, scale):
    qi = pl.program_id(2)
    q = q_ref[0]
    ri = lax.broadcasted_iota(jnp.int32, (blk, blk), 0)
    ci = lax.broadcasted_iota(jnp.int32, (blk, blk), 1)
    after = (ri > ci).astype(BF16)

    def block(kb, masked):
        k0 = pl.multiple_of(kb * blk, blk)
        kblk = k_ref[0, pl.ds(k0, blk), :]
        vblk = v_ref[0, pl.ds(k0, blk), :]
        z = _dot_nt(q, kblk) * scale
        lse = jnp.log(1.0 + jnp.exp(-jnp.abs(z)))
        log_p = jnp.minimum(z, 0.0) - lse
        log_1m = -(jnp.maximum(z, 0.0) + lse)
        if masked:
            valid = ci < ri
            log_1m = jnp.where(valid, log_1m, 0.0)
        hi, lo = _split2(log_1m)
        between = _dot(hi, after) + _dot(lo, after)
        wts = jnp.exp(log_p + between + run_ref[...])
        if masked:
            wts = jnp.where(valid, wts, 0.0)
        acc_ref[...] += _dot(wts.astype(BF16), vblk)
        run_ref[...] += jnp.sum(log_1m, axis=1, keepdims=True)

    acc_ref[...] = jnp.zeros(acc_ref.shape, F32)
    run_ref[...] = jnp.zeros(run_ref.shape, F32)
    block(qi, True)

    def body(i, carry):
        block(qi - 1 - i, False)
        return carry

    lax.fori_loop(0, qi, body, 0)
    o_ref[0] = acc_ref[...].astype(o_ref.dtype)


def _sb_attention(qkv, heads):
    bsz, seq, _ = qkv.shape
    d = SB_HEAD_DIM
    blk = min(SB_BLOCK, seq)
    return pl.pallas_call(
        functools.partial(_sb_kernel, blk=blk, scale=d ** -0.5),
        grid=(bsz, heads, seq // blk),
        in_specs=[pl.BlockSpec((1, blk, d), lambda b, h, i: (b, i, h)),
                  pl.BlockSpec((1, seq, d), lambda b, h, i: (b, 0, heads + h)),
                  pl.BlockSpec((1, seq, d), lambda b, h, i: (b, 0, 2 * heads + h))],
        out_specs=pl.BlockSpec((1, blk, d), lambda b, h, i: (b, i, h)),
        out_shape=jax.ShapeDtypeStruct((bsz, seq, heads * d), BF16),
        scratch_shapes=[pltpu.VMEM((blk, d), F32), pltpu.VMEM((blk, 1), F32)],
        compiler_params=_cparams("arbitrary", "arbitrary", "arbitrary"),
        name="stick_breaking_attention",
    )(qkv, qkv, qkv)


def _gelu_tanh(x):
    return 0.5 * x * (1.0 + jnp.tanh(0.7978845608028654 * (x + 0.044715 * (x * x * x))))


def _mix_ffn_kernel(*refs, even, final, f_chunks):
    if even:
        (h_ref, o_ref, y_ref, gw_ref, gb_ref, wtop_ref, wbot_ref, gtm_ref, nw_ref, sc_ref, sh_ref, gtf_ref,
         win_ref, wout_ref, fw_ref, out_ref) = refs
        y = _gelu_tanh(y_ref[0])
        y = y * _sigmoid(_dot(y.astype(BF16), gw_ref[...]) + gb_ref[...])
        mix = _dot(o_ref[0], wtop_ref[...]) + _dot(y.astype(BF16), wbot_ref[...])
    else:
        (h_ref, o_ref, wtop_ref, gtm_ref, nw_ref, sc_ref, sh_ref, gtf_ref,
         win_ref, wout_ref, fw_ref, out_ref) = refs
        mix = _dot(o_ref[0], wtop_ref[...])
    h1 = h_ref[0] + gtm_ref[0] * mix
    hn = _norm_mod(h1, nw_ref[...], sc_ref[0], sh_ref[0]).astype(BF16)
    hidden = wout_ref.shape[0]
    fc = hidden // f_chunks
    acc = None
    for c in range(f_chunks):
        gate = _dot(hn, win_ref[:, c * fc:(c + 1) * fc])
        up = _dot(hn, win_ref[:, hidden + c * fc:hidden + (c + 1) * fc])
        part = _dot((_silu(gate) * up).astype(BF16), wout_ref[c * fc:(c + 1) * fc, :])
        acc = part if acc is None else acc + part
    h2 = h1 + gtf_ref[0] * acc
    if final:
        h2 = _rms(h2) * fw_ref[...]
    out_ref[0] = h2


def _mix_ffn(h, acts, mix_w, glu, gt_m, nw, sc, sh, gt_f, w_in, w_out, final_w, *, even, final):
    bsz, seq, d = h.shape
    tm = min(ROW_TILE, seq)
    hidden = w_out.shape[0]
    row = lambda b, t: (b, t, 0)
    per_b = lambda b, t: (b, 0, 0)
    vec_d = pl.BlockSpec((1, d), lambda b, t: (0, 0))
    mod_spec = pl.BlockSpec((1, 1, d), per_b)
    args, specs = [h], [pl.BlockSpec((1, tm, d), row)]
    if even:
        o, y = acts
        half = o.shape[-1]
        glu_w, glu_b = glu
        args += [o, y, glu_w.astype(BF16), glu_b.astype(F32).reshape(1, -1),
                 mix_w[:half].astype(BF16), mix_w[half:].astype(BF16)]
        specs += [pl.BlockSpec((1, tm, half), row), pl.BlockSpec((1, tm, y.shape[-1]), row),
                  _const_spec(glu_w.shape), pl.BlockSpec((1, glu_b.shape[-1]), lambda b, t: (0, 0)),
                  _const_spec((half, d)), _const_spec((d - half, d))]
    else:
        (o,) = acts
        args += [o, mix_w.astype(BF16)]
        specs += [pl.BlockSpec((1, tm, d), row), _const_spec((d, d))]
    args += [gt_m, nw.astype(F32).reshape(1, d), sc, sh, gt_f, w_in.astype(BF16), w_out.astype(BF16),
             final_w.astype(F32).reshape(1, d)]
    specs += [mod_spec, vec_d, mod_spec, mod_spec, mod_spec, _const_spec(w_in.shape), _const_spec(w_out.shape), vec_d]
    return pl.pallas_call(
        functools.partial(_mix_ffn_kernel, even=even, final=final, f_chunks=2),
        grid=(bsz, seq // tm),
        in_specs=specs,
        out_specs=pl.BlockSpec((1, tm, d), row),
        out_shape=jax.ShapeDtypeStruct((bsz, seq, d), F32),
        compiler_params=_cparams("arbitrary", "arbitrary"),
        name="outproj_swiglu_ffn",
    )(*args)


def kernel(x, c, ada_w, ada_b, norm_mix_w, norm_ffn_w, ffn_w_in, ffn_w_out, hy_w_in, hy_conv_w, hy_a_log, hy_dt_bias, hy_head_norm_w, s5_lam_re, s5_lam_im, s5_log_dt, s5_b_re, s5_b_im, s5_c_re, s5_c_im, s5_d, s5_glu_w, s5_glu_b, hy_w_out, sb_w_in, sb_w_out, final_norm_w):
    bsz, seq, d = x.shape
    depth = ada_w.shape[0]
    gdn_width = d // 2
    gdn_heads = gdn_width // GDN_HEAD_DIM
    s5_width = d - gdn_width
    sb_heads = d // SB_HEAD_DIM

    mod = _modulation(c, ada_w, ada_b)
    h = x
    for i in range(depth):
        sh_m, sc_m, gt_m, sh_f, sc_f, gt_f = (mod[i, :bsz, k * d:(k + 1) * d].reshape(bsz, 1, d) for k in range(6))
        j = i // 2
        last = i == depth - 1
        if i % 2 == 0:
            w = hy_w_in[j]
            w4 = 4 * gdn_width
            gate_cols = jnp.pad(w[:, w4:w4 + 2 * gdn_heads], ((0, 0), (0, LANES - 2 * gdn_heads)))
            w_cat = jnp.concatenate([w[:, :w4], gate_cols, w[:, w4 + 2 * gdn_heads:]], axis=1).astype(BF16)
            qkv_pre, zg, ba, u = _inproj(h, norm_mix_w[i], sc_m, sh_m, w_cat,
                                         (3 * gdn_width, gdn_width, LANES, s5_width), (F32, F32, F32, F32))
            o = _gdn(qkv_pre, zg, ba, hy_conv_w[j], hy_a_log[j], hy_dt_bias[j], hy_head_norm_w[j])
            y = _s5(u, s5_lam_re[j], s5_lam_im[j], s5_log_dt[j], s5_b_re[j], s5_b_im[j],
                    s5_c_re[j], s5_c_im[j], s5_d[j])
            h = _mix_ffn(h, (o, y), hy_w_out[j], (s5_glu_w[j], s5_glu_b[j]), gt_m, norm_ffn_w[i], sc_f, sh_f, gt_f,
                         ffn_w_in[i], ffn_w_out[i], final_norm_w, even=True, final=last)
        else:
            (qkv,) = _inproj(h, norm_mix_w[i], sc_m, sh_m, sb_w_in[j].astype(BF16), (3 * d,), (BF16,))
            o = _sb_attention(qkv, sb_heads)
            h = _mix_ffn(h, (o,), sb_w_out[j], None, gt_m, norm_ffn_w[i], sc_f, sh_f, gt_f,
                         ffn_w_in[i], ffn_w_out[i], final_norm_w, even=False, final=last)
    return h
```

```python
import functools

import jax
import jax.numpy as jnp
from jax import lax
from jax.experimental import pallas as pl
from jax.experimental.pallas import tpu as pltpu

F32 = jnp.float32
BF16 = jnp.bfloat16
NORM_EPS = 1e-6

GDN_HEAD_DIM = 128
CONV_K = 4
GDN_CHUNK = 64
S5_GROUP = 16
S5_STATE = 64
SB_HEAD_DIM = 128

LANES = 128
SUBLANES = 8
VMEM_LIMIT_BYTES = 56 * 1024 * 1024

ROW_TILE = 512
GDN_TILE = 256
S5_CHUNK = 16
SB_BLOCK = 256
SB_Q_TILE = 1024
LOG2_E = 1.4426950408889634
MOD_COL_TILE = 1536


def _cparams(*sem):
    return pltpu.CompilerParams(dimension_semantics=sem, vmem_limit_bytes=VMEM_LIMIT_BYTES)


def _const_spec(shape):
    nd = len(shape)
    return pl.BlockSpec(shape, lambda *_: (0,) * nd, pipeline_mode=pl.Buffered(1))


def _dot(a, b):
    return jnp.dot(a, b, preferred_element_type=F32)


def _dot_nt(a, b):
    return lax.dot_general(a, b, (((1,), (1,)), ((), ())), preferred_element_type=F32)


def _dot_tn(a, b):
    return lax.dot_general(a, b, (((0,), (0,)), ((), ())), preferred_element_type=F32)


def _split2(x):
    hi = x.astype(BF16)
    lo = (x - hi.astype(F32)).astype(BF16)
    return hi, lo


def _mm3(a, b):
    ah, al = _split2(a)
    bh, bl = _split2(b)
    return _dot(ah, bh) + (_dot(ah, bl) + _dot(al, bh))


def _mm3_nt(a, b):
    ah, al = _split2(a)
    bh, bl = _split2(b)
    return _dot_nt(ah, bh) + (_dot_nt(ah, bl) + _dot_nt(al, bh))


def _sigmoid(x):
    return 1.0 / (1.0 + jnp.exp(-x))


def _silu(x):
    return x * _sigmoid(x)


def _softplus(x):
    return jnp.maximum(x, 0.0) + jnp.log(1.0 + jnp.exp(-jnp.abs(x)))


def _rms(x):
    return x * lax.rsqrt(jnp.mean(x * x, axis=-1, keepdims=True) + NORM_EPS)


def _norm_mod(x, nw, sc, sh):
    return (_rms(x) * nw) * (1.0 + sc) + sh


def _mod_kernel(c_ref, w_ref, b_ref, o_ref):
    ca = _silu(c_ref[...])
    o_ref[0] = _mm3(ca, w_ref[0]) + b_ref[0]


def _modulation(c, ada_w, ada_b):
    depth, d, n = ada_w.shape
    bsz = c.shape[0]
    rows = -(-bsz // SUBLANES) * SUBLANES
    c_pad = jnp.pad(c, ((0, rows - bsz), (0, 0)))
    tn = MOD_COL_TILE
    return pl.pallas_call(
        _mod_kernel,
        grid=(depth, n // tn),
        in_specs=[pl.BlockSpec((rows, d), lambda i, j: (0, 0)),
                  pl.BlockSpec((1, d, tn), lambda i, j: (i, 0, j)),
                  pl.BlockSpec((1, 1, tn), lambda i, j: (i, 0, j))],
        out_specs=pl.BlockSpec((1, rows, tn), lambda i, j: (i, 0, j)),
        out_shape=jax.ShapeDtypeStruct((depth, rows, n), F32),
        compiler_params=_cparams("arbitrary", "arbitrary"),
        name="adaln_modulation",
    )(c_pad, ada_w, ada_b.reshape(depth, 1, n))


def _inproj_kernel(h_ref, nw_ref, sc_ref, sh_ref, w_ref, *out_refs, widths, scales):
    hn = _norm_mod(h_ref[0], nw_ref[...], sc_ref[0], sh_ref[0]).astype(BF16)
    off = 0
    for o_ref, wd, s in zip(out_refs, widths, scales):
        acc = _dot(hn, w_ref[:, off:off + wd])
        if s != 1.0:
            acc = acc * s
        o_ref[0] = acc.astype(o_ref.dtype)
        off += wd


def _inproj(h, nw, sc, sh, w, widths, dtypes, scales=None):
    scales = scales or (1.0,) * len(widths)
    bsz, seq, d = h.shape
    tm = min(ROW_TILE, seq)
    n = w.shape[1]
    row = lambda b, t: (b, t, 0)
    per_b = lambda b, t: (b, 0, 0)
    return pl.pallas_call(
        functools.partial(_inproj_kernel, widths=widths, scales=scales),
        grid=(bsz, seq // tm),
        in_specs=[pl.BlockSpec((1, tm, d), row),
                  pl.BlockSpec((1, d), lambda b, t: (0, 0)),
                  pl.BlockSpec((1, 1, d), per_b),
                  pl.BlockSpec((1, 1, d), per_b),
                  _const_spec((d, n))],
        out_specs=[pl.BlockSpec((1, tm, wd), row) for wd in widths],
        out_shape=[jax.ShapeDtypeStruct((bsz, seq, wd), dt) for wd, dt in zip(widths, dtypes)],
        compiler_params=_cparams("arbitrary", "arbitrary"),
        name="norm_mod_inproj",
    )(h, nw.reshape(1, d), sc, sh, w)


def _unit_lower_inverses(mats, ri, ci):
    n = mats[0].shape[0]
    eye = (ri == ci).astype(F32)
    pair = (ri // 2 == ci // 2) & (ri > ci)
    ts = [eye - jnp.where(pair, a, 0.0) for a in mats]
    s = 2
    while s < n:
        sel = (ri // (2 * s) == ci // (2 * s)) & (ri // s > ci // s)
        tmps = [_mm3(t, jnp.where(sel, a, 0.0)) for t, a in zip(ts, mats)]
        ts = [t - _mm3(tmp, t) for t, tmp in zip(ts, tmps)]
        s *= 2
    return ts


def _gdn_kernel(x_ref, z_ref, ba_ref, cw_ref, garow_ref, gbrow_ref, hnw_ref, o_ref,
                xbuf_ref, state_ref, *, heads, tile):
    dh = GDN_HEAD_DIM
    width = heads * dh
    ck = GDN_CHUNK
    nck = tile // ck
    t_id = pl.program_id(1)

    @pl.when(t_id == 0)
    def _():
        xbuf_ref[0:SUBLANES, :] = jnp.zeros((SUBLANES, 3 * width), F32)
        state_ref[...] = jnp.zeros(state_ref.shape, F32)

    x = x_ref[0]
    xbuf_ref[SUBLANES:SUBLANES + tile, :] = x
    conv = cw_ref[0:1, :] * xbuf_ref[pl.ds(SUBLANES - CONV_K + 1, tile), :]
    for k in range(1, CONV_K):
        conv = conv + cw_ref[k:k + 1, :] * xbuf_ref[pl.ds(SUBLANES - CONV_K + 1 + k, tile), :]
    xbuf_ref[0:SUBLANES, :] = x[tile - SUBLANES:tile, :]
    qkv = _silu(conv)

    ba = ba_ref[0]
    beta_all = _sigmoid(ba)
    g_all = -jnp.exp(garow_ref[...]) * _softplus(ba + gbrow_ref[...])
    ri_t = lax.broadcasted_iota(jnp.int32, (tile, tile), 0)
    ci_t = lax.broadcasted_iota(jnp.int32, (tile, tile), 1)
    cum_mat = ((ri_t // ck == ci_t // ck) & (ri_t >= ci_t)).astype(BF16)
    g_hi = g_all.astype(BF16)
    g_r1 = g_all - g_hi.astype(F32)
    g_mid = g_r1.astype(BF16)
    g_lo = (g_r1 - g_mid.astype(F32)).astype(BF16)
    gc_all = _dot(cum_mat, g_hi) + (_dot(cum_mat, g_mid) + _dot(cum_mat, g_lo))
    gc_all_t = gc_all.T

    ri = lax.broadcasted_iota(jnp.int32, (ck, ck), 0)
    ci = lax.broadcasted_iota(jnp.int32, (ck, ck), 1)
    hnw = hnw_ref[...]
    z_all = z_ref[0]

    prep = []
    for n in range(nck):
        sl = slice(n * ck, (n + 1) * ck)
        for h in range(heads):
            q = qkv[sl, h * dh:(h + 1) * dh]
            k = qkv[sl, width + h * dh:width + (h + 1) * dh]
            v = qkv[sl, 2 * width + h * dh:2 * width + (h + 1) * dh]
            q = q * lax.rsqrt(jnp.sum(q * q, axis=-1, keepdims=True) + NORM_EPS) * (dh ** -0.5)
            k = k * lax.rsqrt(jnp.sum(k * k, axis=-1, keepdims=True) + NORM_EPS)
            beta = beta_all[sl, h:h + 1]
            gcc = gc_all[sl, heads + h:heads + h + 1]
            gcr = gc_all_t[heads + h:heads + h + 1, sl]
            g_last = gcc[ck - 1:ck, :]
            decay = jnp.where(ri >= ci, jnp.exp(jnp.minimum(gcc - gcr, 0.0)), 0.0)
            k_beta = k * beta
            kb16, k16 = k_beta.astype(BF16), k.astype(BF16)
            egc = jnp.exp(gcc)
            prep.append(dict(
                a=jnp.where(ri > ci, _dot_nt(kb16, k16) * decay, 0.0),
                rhs=jnp.concatenate([v * beta, k_beta * egc], axis=-1),
                attn=(_dot_nt(q.astype(BF16), k16) * decay).astype(BF16),
                q_dec=(q * egc).astype(BF16),
                k_dec=(k * jnp.exp(g_last - gcc)).astype(BF16),
                s_decay=jnp.exp(g_last)))
    invs = _unit_lower_inverses([p["a"] for p in prep], ri, ci)
    sols = [_mm3(t, p["rhs"]) for t, p in zip(invs, prep)]

    states = [state_ref[h] for h in range(heads)]
    outs = [[] for _ in range(heads)]
    for n in range(nck):
        cur = [(prep[n * heads + h], sols[n * heads + h]) for h in range(heads)]
        s16 = [st.astype(BF16) for st in states]
        vn16 = [(sol[:, :dh] - _dot(sol[:, dh:].astype(BF16), s)).astype(BF16) for (_, sol), s in zip(cur, s16)]
        for h in range(heads):
            p = cur[h][0]
            outs[h].append(_dot(p["q_dec"], s16[h]) + _dot(p["attn"], vn16[h]))
        states = [st * cur[h][0]["s_decay"] + _dot_tn(cur[h][0]["k_dec"], vn16[h]) for h, st in enumerate(states)]
    for h in range(heads):
        state_ref[h] = states[h]
        o_h = jnp.concatenate(outs[h], axis=0) if nck > 1 else outs[h][0]
        o_h = (_rms(o_h) * hnw) * _silu(z_all[:, h * dh:(h + 1) * dh])
        o_ref[0, :, h * dh:(h + 1) * dh] = o_h.astype(o_ref.dtype)


def _gdn(qkv_pre, z, ba, conv_w, a_log, dt_bias, head_norm_w):
    bsz, seq, w3 = qkv_pre.shape
    width = w3 // 3
    heads = width // GDN_HEAD_DIM
    tile = min(GDN_TILE, seq)
    garow = jnp.zeros((1, LANES), F32).at[0, heads:2 * heads].set(a_log.astype(F32))
    gbrow = jnp.zeros((1, LANES), F32).at[0, heads:2 * heads].set(dt_bias.astype(F32))
    row = lambda b, t: (b, t, 0)
    fix2 = lambda b, t: (0, 0)
    return pl.pallas_call(
        functools.partial(_gdn_kernel, heads=heads, tile=tile),
        grid=(bsz, seq // tile),
        in_specs=[pl.BlockSpec((1, tile, w3), row),
                  pl.BlockSpec((1, tile, width), row),
                  pl.BlockSpec((1, tile, LANES), row),
                  pl.BlockSpec((CONV_K, w3), fix2),
                  pl.BlockSpec((1, LANES), fix2),
                  pl.BlockSpec((1, LANES), fix2),
                  pl.BlockSpec((1, GDN_HEAD_DIM), fix2)],
        out_specs=pl.BlockSpec((1, tile, width), row),
        out_shape=jax.ShapeDtypeStruct((bsz, seq, width), BF16),
        scratch_shapes=[pltpu.VMEM((SUBLANES + tile, w3), F32),
                        pltpu.VMEM((heads, GDN_HEAD_DIM, GDN_HEAD_DIM), F32)],
        compiler_params=_cparams("arbitrary", "arbitrary"),
        name="gated_deltanet",
    )(qkv_pre, z, ba, conv_w.astype(F32), garow, gbrow, head_norm_w.astype(F32).reshape(1, -1))


def _s5_prep_kernel(lr_ref, li_ref, ldt_ref, btr_ref, bti_ref, cr_ref, ci_ref,
                    kst_ref, bjr_ref, bji_ref, cor_ref, coi_ref, l16r_ref, l16i_ref):
    nt = S5_CHUNK
    lr, li = lr_ref[0], li_ref[0]
    dt = jnp.exp(ldt_ref[0])
    mag = jnp.exp(lr * dt)
    lb_re, lb_im = mag * jnp.cos(li * dt), mag * jnp.sin(li * dt)
    den = lr * lr + li * li
    num_re, num_im = lb_re - 1.0, lb_im
    f_re = (num_re * lr + num_im * li) / den
    f_im = (num_im * lr - num_re * li) / den
    btr, bti = btr_ref[0], bti_ref[0]
    bb_re = f_re * btr - f_im * bti
    bb_im = f_re * bti + f_im * btr
    cr, ci = cr_ref[0], ci_ref[0]
    p_re, p_im = [jnp.ones_like(lr)], [jnp.zeros_like(lr)]
    for _ in range(nt):
        p_re.append(p_re[-1] * lb_re - p_im[-1] * lb_im)
        p_im.append(p_re[-2] * lb_im + p_im[-1] * lb_re)
    cp_re = [cr * pr - ci * pi for pr, pi in zip(p_re, p_im)]
    cp_im = [cr * pi + ci * pr for pr, pi in zip(p_re, p_im)]
    kst_ref[0] = (_mm3_nt(jnp.concatenate(cp_re[:nt], axis=0), bb_re)
                  - _mm3_nt(jnp.concatenate(cp_im[:nt], axis=0), bb_im))
    bjr_ref[0] = jnp.concatenate([bb_re * p_re[nt - 1 - s] - bb_im * p_im[nt - 1 - s] for s in range(nt)], axis=0)
    bji_ref[0] = jnp.concatenate([bb_re * p_im[nt - 1 - s] + bb_im * p_re[nt - 1 - s] for s in range(nt)], axis=0)
    cor_ref[0] = jnp.concatenate(cp_re[1:nt + 1], axis=0)
    coi_ref[0] = jnp.concatenate(cp_im[1:nt + 1], axis=0)
    l16r_ref[0] = p_re[nt]
    l16i_ref[0] = p_im[nt]


def _s5_prep(lam_re, lam_im, log_dt, b_re, b_im, c_re, c_im):
    g, p = lam_re.shape
    hs = c_re.shape[1]
    nt = S5_CHUNK
    vec = pl.BlockSpec((1, 1, p), lambda i: (i, 0, 0))
    mat = pl.BlockSpec((1, hs, p), lambda i: (i, 0, 0))
    big = pl.BlockSpec((1, nt * hs, p), lambda i: (i, 0, 0))
    f32 = lambda a: a.astype(F32)
    return pl.pallas_call(
        _s5_prep_kernel,
        grid=(g,),
        in_specs=[vec, vec, pl.BlockSpec((1, 1, 1), lambda i: (i, 0, 0)), mat, mat, mat, mat],
        out_specs=[pl.BlockSpec((1, nt * hs, hs), lambda i: (i, 0, 0)), big, big, big, big, vec, vec],
        out_shape=[jax.ShapeDtypeStruct((g, nt * hs, hs), F32)]
                  + [jax.ShapeDtypeStruct((g, nt * hs, p), F32)] * 4
                  + [jax.ShapeDtypeStruct((g, 1, p), F32)] * 2,
        compiler_params=_cparams("arbitrary"),
        name="s5_discretise",
    )(f32(lam_re).reshape(g, 1, p), f32(lam_im).reshape(g, 1, p), f32(log_dt).reshape(g, 1, 1),
      f32(b_re).transpose(0, 2, 1), f32(b_im).transpose(0, 2, 1), f32(c_re), f32(c_im))


def _s5_kernel(u_ref, toep_ref, binj_ref, ct_ref, dt_ref, l16r_ref, l16i_ref, y_ref,
               sre_ref, sim_ref, xre_ref, xim_ref, *, streams):
    rows = u_ref.shape[1]
    steps = rows // streams
    u = u_ref[0]
    ub = u.astype(BF16)
    inj = _dot(ub, binj_ref[0])
    sre_ref[...] = inj[:, :LANES]
    sim_ref[...] = inj[:, LANES:]
    lr = jnp.broadcast_to(l16r_ref[0], (streams, LANES))
    li = jnp.broadcast_to(l16i_ref[0], (streams, LANES))

    def scan_step(i, carry):
        xr, xi = carry
        r0 = pl.multiple_of(i * streams, streams)
        xre_ref[pl.ds(r0, streams), :] = xr
        xim_ref[pl.ds(r0, streams), :] = xi
        sr = sre_ref[pl.ds(r0, streams), :]
        si = sim_ref[pl.ds(r0, streams), :]
        return lr * xr - li * xi + sr, lr * xi + li * xr + si

    zero = jnp.zeros((streams, LANES), F32)
    xr_end, xi_end = lax.fori_loop(0, steps, scan_step, (zero, zero), unroll=8)

    odd = (lax.broadcasted_iota(jnp.int32, (streams, LANES), 0) % 2) == 1
    zr0 = jnp.where(odd, pltpu.roll(xr_end, 1, axis=0), 0.0)
    zi0 = jnp.where(odd, pltpu.roll(xi_end, 1, axis=0), 0.0)

    def carry_step(i, carry):
        zr, zi = carry
        r0 = pl.multiple_of(i * streams, streams)
        xre_ref[pl.ds(r0, streams), :] = xre_ref[pl.ds(r0, streams), :] + zr
        xim_ref[pl.ds(r0, streams), :] = xim_ref[pl.ds(r0, streams), :] + zi
        return lr * zr - li * zi, lr * zi + li * zr

    lax.fori_loop(0, steps, carry_step, (zr0, zi0), unroll=8)

    xin = jnp.concatenate([xre_ref[...], xim_ref[...]], axis=1).astype(BF16)
    y_ref[0] = _dot(ub, toep_ref[0]) + _dot(xin, ct_ref[0]) + dt_ref[0] * u


def _s5(u, lam_re, lam_im, log_dt, b_re, b_im, c_re, c_im, d_skip):
    bsz, seq, width = u.shape
    hs, p, nt = S5_GROUP, S5_STATE, S5_CHUNK
    g = width // hs
    streams = 2 * bsz
    assert streams == SUBLANES and p * 2 == LANES and nt * hs == 2 * LANES
    steps = seq // (2 * nt)
    rows = steps * streams
    kst, bjr, bji, cor, coi, l16r, l16i = _s5_prep(lam_re, lam_im, log_dt, b_re, b_im, c_re, c_im)

    s_idx = jnp.arange(nt)
    tau = s_idx[None, :] - s_idx[:, None]
    k4 = kst.reshape(g, nt, hs, hs)
    toep = jnp.where((tau >= 0)[None, :, :, None, None], k4[:, jnp.clip(tau, 0, nt - 1)], 0.0)
    toep = toep.transpose(0, 1, 4, 2, 3).reshape(g, nt * hs, nt * hs).astype(BF16)
    zpad = jnp.zeros((g, nt * hs, LANES - p), F32)
    binj = jnp.concatenate([bjr, zpad, bji, zpad], axis=-1).astype(BF16)
    zrow = jnp.zeros((g, LANES - p, nt * hs), F32)
    ct = jnp.concatenate([cor.transpose(0, 2, 1), zrow, -coi.transpose(0, 2, 1), zrow], axis=1).astype(BF16)
    dtile = jnp.tile(d_skip.astype(F32).reshape(g, 1, hs), (1, 1, nt))
    lpad = jnp.zeros((g, 1, LANES - p), F32)
    l16r = jnp.concatenate([l16r, lpad], axis=-1)
    l16i = jnp.concatenate([l16i, lpad], axis=-1)

    ut = u.reshape(bsz, 2, steps, nt, g, hs).transpose(4, 2, 0, 1, 3, 5).reshape(g, rows, nt * hs)
    per_g3 = lambda i: (i, 0, 0)
    y = pl.pallas_call(
        functools.partial(_s5_kernel, streams=streams),
        grid=(g,),
        in_specs=[pl.BlockSpec((1, rows, nt * hs), per_g3),
                  pl.BlockSpec((1, nt * hs, nt * hs), per_g3),
                  pl.BlockSpec((1, nt * hs, 2 * LANES), per_g3),
                  pl.BlockSpec((1, 2 * LANES, nt * hs), per_g3),
                  pl.BlockSpec((1, 1, nt * hs), per_g3),
                  pl.BlockSpec((1, 1, LANES), per_g3),
                  pl.BlockSpec((1, 1, LANES), per_g3)],
        out_specs=pl.BlockSpec((1, rows, nt * hs), per_g3),
        out_shape=jax.ShapeDtypeStruct((g, rows, nt * hs), F32),
        scratch_shapes=[pltpu.VMEM((rows, LANES), F32)] * 4,
        compiler_params=_cparams("arbitrary"),
        name="s5_chunked_scan",
    )(ut, toep, binj, ct, dtile, l16r, l16i)
    return y.reshape(g, steps, bsz, 2, nt, hs).transpose(2, 3, 1, 4, 0, 5).reshape(bsz, seq, width)


def _sb_kernel(q_ref, k_ref, v_ref, o_ref, acc_ref, run_ref, *, blk, nsub):
    qi = pl.program_id(2)
    ri = lax.broadcasted_iota(jnp.int32, (blk, blk), 0)
    ci = lax.broadcasted_iota(jnp.int32, (blk, blk), 1)
    after = (ri > ci).astype(BF16)
    valid = ci < ri

    def load_kv(kb):
        k0 = pl.multiple_of(kb * blk, blk)
        return k_ref[0, pl.ds(k0, blk), :], v_ref[0, pl.ds(k0, blk), :]

    def pre(sub, kblk, masked):
        s = _dot_nt(q_ref[0, sub * blk:(sub + 1) * blk, :], kblk)
        lse = jnp.log2(1.0 + jnp.exp2(-jnp.abs(s)))
        log_p = jnp.minimum(s, 0.0) - lse
        neg_log_1m = jnp.maximum(s, 0.0) + lse
        if masked:
            neg_log_1m = jnp.where(valid, neg_log_1m, 0.0)
        hi, lo = _split2(neg_log_1m)
        later = _dot(hi, after) + _dot(lo, after)
        return log_p - later, jnp.sum(neg_log_1m, axis=1, keepdims=True)

    def post(sub, part, tot, vblk, masked):
        rows = slice(sub * blk, (sub + 1) * blk)
        wts = jnp.exp2(part - run_ref[rows, :])
        if masked:
            wts = jnp.where(valid, wts, 0.0)
        acc_ref[rows, :] += _dot(wts.astype(BF16), vblk)
        run_ref[rows, :] += tot

    def sweep(kb, items):
        kblk, vblk = load_kv(kb)
        pres = [pre(sub, kblk, masked) for sub, masked in items]
        for (sub, masked), (part, tot) in zip(items, pres):
            post(sub, part, tot, vblk, masked)

    acc_ref[...] = jnp.zeros(acc_ref.shape, F32)
    run_ref[...] = jnp.zeros(run_ref.shape, F32)
    top = nsub * qi
    for j in reversed(range(nsub)):
        sweep(top + j, [(sub, sub == j) for sub in range(j, nsub)])

    def body(i, carry):
        sweep(top - 1 - i, [(sub, False) for sub in range(nsub)])
        return carry

    lax.fori_loop(0, top, body, 0)
    o_ref[0] = acc_ref[...].astype(o_ref.dtype)


def _sb_attention(q, kv, heads):
    bsz, seq, _ = q.shape
    d = SB_HEAD_DIM
    blk = min(SB_BLOCK, seq)
    nsub = max(1, min(SB_Q_TILE, seq) // blk)
    tq = nsub * blk
    return pl.pallas_call(
        functools.partial(_sb_kernel, blk=blk, nsub=nsub),
        grid=(bsz, heads, seq // tq),
        in_specs=[pl.BlockSpec((1, tq, d), lambda b, h, i: (b, i, h)),
                  pl.BlockSpec((1, seq, d), lambda b, h, i: (b, 0, h)),
                  pl.BlockSpec((1, seq, d), lambda b, h, i: (b, 0, heads + h))],
        out_specs=pl.BlockSpec((1, tq, d), lambda b, h, i: (b, i, h)),
        out_shape=jax.ShapeDtypeStruct((bsz, seq, heads * d), BF16),
        scratch_shapes=[pltpu.VMEM((tq, d), F32), pltpu.VMEM((tq, 1), F32)],
        compiler_params=_cparams("arbitrary", "arbitrary", "arbitrary"),
        name="stick_breaking_attention",
    )(q, kv, kv)


def _gelu_tanh(x):
    return 0.5 * x * (1.0 + jnp.tanh(0.7978845608028654 * (x + 0.044715 * (x * x * x))))


def _mix_ffn_kernel(*refs, even, final, f_chunks):
    if even:
        (h_ref, o_ref, y_ref, gw_ref, gb_ref, wtop_ref, wbot_ref, gtm_ref, nw_ref, sc_ref, sh_ref, gtf_ref,
         win_ref, wout_ref, fw_ref, out_ref) = refs
        y = _gelu_tanh(y_ref[0])
        y = y * _sigmoid(_dot(y.astype(BF16), gw_ref[...]) + gb_ref[...])
        mix = _dot(o_ref[0], wtop_ref[...]) + _dot(y.astype(BF16), wbot_ref[...])
    else:
        (h_ref, o_ref, wtop_ref, gtm_ref, nw_ref, sc_ref, sh_ref, gtf_ref,
         win_ref, wout_ref, fw_ref, out_ref) = refs
        mix = _dot(o_ref[0], wtop_ref[...])
    h1 = h_ref[0] + gtm_ref[0] * mix
    hn = _norm_mod(h1, nw_ref[...], sc_ref[0], sh_ref[0]).astype(BF16)
    hidden = wout_ref.shape[0]
    fc = hidden // f_chunks
    acc = None
    for c in range(f_chunks):
        gate = _dot(hn, win_ref[:, c * fc:(c + 1) * fc])
        up = _dot(hn, win_ref[:, hidden + c * fc:hidden + (c + 1) * fc])
        part = _dot((_silu(gate) * up).astype(BF16), wout_ref[c * fc:(c + 1) * fc, :])
        acc = part if acc is None else acc + part
    h2 = h1 + gtf_ref[0] * acc
    if final:
        h2 = _rms(h2) * fw_ref[...]
    out_ref[0] = h2


def _mix_ffn(h, acts, mix_w, glu, gt_m, nw, sc, sh, gt_f, w_in, w_out, final_w, *, even, final):
    bsz, seq, d = h.shape
    tm = min(ROW_TILE, seq)
    hidden = w_out.shape[0]
    row = lambda b, t: (b, t, 0)
    per_b = lambda b, t: (b, 0, 0)
    vec_d = pl.BlockSpec((1, d), lambda b, t: (0, 0))
    mod_spec = pl.BlockSpec((1, 1, d), per_b)
    args, specs = [h], [pl.BlockSpec((1, tm, d), row)]
    if even:
        o, y = acts
        half = o.shape[-1]
        glu_w, glu_b = glu
        args += [o, y, glu_w.astype(BF16), glu_b.astype(F32).reshape(1, -1),
                 mix_w[:half].astype(BF16), mix_w[half:].astype(BF16)]
        specs += [pl.BlockSpec((1, tm, half), row), pl.BlockSpec((1, tm, y.shape[-1]), row),
                  _const_spec(glu_w.shape), pl.BlockSpec((1, glu_b.shape[-1]), lambda b, t: (0, 0)),
                  _const_spec((half, d)), _const_spec((d - half, d))]
    else:
        (o,) = acts
        args += [o, mix_w.astype(BF16)]
        specs += [pl.BlockSpec((1, tm, d), row), _const_spec((d, d))]
    args += [gt_m, nw.astype(F32).reshape(1, d), sc, sh, gt_f, w_in.astype(BF16), w_out.astype(BF16),
             final_w.astype(F32).reshape(1, d)]
    specs += [mod_spec, vec_d, mod_spec, mod_spec, mod_spec, _const_spec(w_in.shape), _const_spec(w_out.shape), vec_d]
    return pl.pallas_call(
        functools.partial(_mix_ffn_kernel, even=even, final=final, f_chunks=2),
        grid=(bsz, seq // tm),
        in_specs=specs,
        out_specs=pl.BlockSpec((1, tm, d), row),
        out_shape=jax.ShapeDtypeStruct((bsz, seq, d), F32),
        compiler_params=_cparams("arbitrary", "arbitrary"),
        name="outproj_swiglu_ffn",
    )(*args)


def kernel(x, c, ada_w, ada_b, norm_mix_w, norm_ffn_w, ffn_w_in, ffn_w_out, hy_w_in, hy_conv_w, hy_a_log, hy_dt_bias, hy_head_norm_w, s5_lam_re, s5_lam_im, s5_log_dt, s5_b_re, s5_b_im, s5_c_re, s5_c_im, s5_d, s5_glu_w, s5_glu_b, hy_w_out, sb_w_in, sb_w_out, final_norm_w):
    bsz, seq, d = x.shape
    depth = ada_w.shape[0]
    gdn_width = d // 2
    gdn_heads = gdn_width // GDN_HEAD_DIM
    s5_width = d - gdn_width
    sb_heads = d // SB_HEAD_DIM

    mod = _modulation(c, ada_w, ada_b)
    h = x
    for i in range(depth):
        sh_m, sc_m, gt_m, sh_f, sc_f, gt_f = (mod[i, :bsz, k * d:(k + 1) * d].reshape(bsz, 1, d) for k in range(6))
        j = i // 2
        last = i == depth - 1
        if i % 2 == 0:
            w = hy_w_in[j]
            w4 = 4 * gdn_width
            gate_cols = jnp.pad(w[:, w4:w4 + 2 * gdn_heads], ((0, 0), (0, LANES - 2 * gdn_heads)))
            w_cat = jnp.concatenate([w[:, :w4], gate_cols, w[:, w4 + 2 * gdn_heads:]], axis=1).astype(BF16)
            qkv_pre, zg, ba, u = _inproj(h, norm_mix_w[i], sc_m, sh_m, w_cat,
                                         (3 * gdn_width, gdn_width, LANES, s5_width), (F32, F32, F32, F32))
            o = _gdn(qkv_pre, zg, ba, hy_conv_w[j], hy_a_log[j], hy_dt_bias[j], hy_head_norm_w[j])
            y = _s5(u, s5_lam_re[j], s5_lam_im[j], s5_log_dt[j], s5_b_re[j], s5_b_im[j],
                    s5_c_re[j], s5_c_im[j], s5_d[j])
            h = _mix_ffn(h, (o, y), hy_w_out[j], (s5_glu_w[j], s5_glu_b[j]), gt_m, norm_ffn_w[i], sc_f, sh_f, gt_f,
                         ffn_w_in[i], ffn_w_out[i], final_norm_w, even=True, final=last)
        else:
            q, kv = _inproj(h, norm_mix_w[i], sc_m, sh_m, sb_w_in[j].astype(BF16), (d, 2 * d), (BF16, BF16),
                            scales=(SB_HEAD_DIM ** -0.5 * LOG2_E, 1.0))
            o = _sb_attention(q, kv, sb_heads)
            h = _mix_ffn(h, (o,), sb_w_out[j], None, gt_m, norm_ffn_w[i], sc_f, sh_f, gt_f,
                         ffn_w_in[i], ffn_w_out[i], final_norm_w, even=False, final=last)
    return h
```

```python
import functools

import jax
import jax.numpy as jnp
from jax import lax
from jax.experimental import pallas as pl
from jax.experimental.pallas import tpu as pltpu

F32 = jnp.float32
BF16 = jnp.bfloat16
NORM_EPS = 1e-6

GDN_HEAD_DIM = 128
CONV_K = 4
GDN_CHUNK = 64
S5_GROUP = 16
S5_STATE = 64
SB_HEAD_DIM = 128

LANES = 128
SUBLANES = 8
VMEM_LIMIT_BYTES = 56 * 1024 * 1024

ROW_TILE = 512
GDN_TILE = 256
S5_CHUNK = 16
SB_BLOCK = 256
SB_Q_TILE = 1024
LOG2_E = 1.4426950408889634
SB_UNDERFLOW_LOG2 = 160.0
MOD_COL_TILE = 1536


def _cparams(*sem):
    return pltpu.CompilerParams(dimension_semantics=sem, vmem_limit_bytes=VMEM_LIMIT_BYTES)


def _const_spec(shape):
    nd = len(shape)
    return pl.BlockSpec(shape, lambda *_: (0,) * nd, pipeline_mode=pl.Buffered(1))


def _dot(a, b):
    return jnp.dot(a, b, preferred_element_type=F32)


def _dot_nt(a, b):
    return lax.dot_general(a, b, (((1,), (1,)), ((), ())), preferred_element_type=F32)


def _dot_tn(a, b):
    return lax.dot_general(a, b, (((0,), (0,)), ((), ())), preferred_element_type=F32)


def _split2(x):
    hi = x.astype(BF16)
    lo = (x - hi.astype(F32)).astype(BF16)
    return hi, lo


def _mm3(a, b):
    ah, al = _split2(a)
    bh, bl = _split2(b)
    return _dot(ah, bh) + (_dot(ah, bl) + _dot(al, bh))


def _mm3_nt(a, b):
    ah, al = _split2(a)
    bh, bl = _split2(b)
    return _dot_nt(ah, bh) + (_dot_nt(ah, bl) + _dot_nt(al, bh))


def _sigmoid(x):
    return 1.0 / (1.0 + jnp.exp(-x))


def _silu(x):
    return x * _sigmoid(x)


def _softplus(x):
    return jnp.maximum(x, 0.0) + jnp.log(1.0 + jnp.exp(-jnp.abs(x)))


def _rms(x):
    return x * lax.rsqrt(jnp.mean(x * x, axis=-1, keepdims=True) + NORM_EPS)


def _norm_mod(x, nw, sc, sh):
    return (_rms(x) * nw) * (1.0 + sc) + sh


def _mod_kernel(c_ref, w_ref, b_ref, o_ref):
    ca = _silu(c_ref[...])
    o_ref[0] = _mm3(ca, w_ref[0]) + b_ref[0]


def _modulation(c, ada_w, ada_b):
    depth, d, n = ada_w.shape
    bsz = c.shape[0]
    rows = -(-bsz // SUBLANES) * SUBLANES
    c_pad = jnp.pad(c, ((0, rows - bsz), (0, 0)))
    tn = MOD_COL_TILE
    return pl.pallas_call(
        _mod_kernel,
        grid=(depth, n // tn),
        in_specs=[pl.BlockSpec((rows, d), lambda i, j: (0, 0)),
                  pl.BlockSpec((1, d, tn), lambda i, j: (i, 0, j)),
                  pl.BlockSpec((1, 1, tn), lambda i, j: (i, 0, j))],
        out_specs=pl.BlockSpec((1, rows, tn), lambda i, j: (i, 0, j)),
        out_shape=jax.ShapeDtypeStruct((depth, rows, n), F32),
        compiler_params=_cparams("arbitrary", "arbitrary"),
        name="adaln_modulation",
    )(c_pad, ada_w, ada_b.reshape(depth, 1, n))


def _inproj_kernel(h_ref, nw_ref, sc_ref, sh_ref, w_ref, *out_refs, widths, scales):
    hn = _norm_mod(h_ref[0], nw_ref[...], sc_ref[0], sh_ref[0]).astype(BF16)
    off = 0
    for o_ref, wd, s in zip(out_refs, widths, scales):
        acc = _dot(hn, w_ref[:, off:off + wd])
        if s != 1.0:
            acc = acc * s
        o_ref[0] = acc.astype(o_ref.dtype)
        off += wd


def _inproj(h, nw, sc, sh, w, widths, dtypes, scales=None):
    scales = scales or (1.0,) * len(widths)
    bsz, seq, d = h.shape
    tm = min(ROW_TILE, seq)
    n = w.shape[1]
    row = lambda b, t: (b, t, 0)
    per_b = lambda b, t: (b, 0, 0)
    return pl.pallas_call(
        functools.partial(_inproj_kernel, widths=widths, scales=scales),
        grid=(bsz, seq // tm),
        in_specs=[pl.BlockSpec((1, tm, d), row),
                  pl.BlockSpec((1, d), lambda b, t: (0, 0)),
                  pl.BlockSpec((1, 1, d), per_b),
                  pl.BlockSpec((1, 1, d), per_b),
                  _const_spec((d, n))],
        out_specs=[pl.BlockSpec((1, tm, wd), row) for wd in widths],
        out_shape=[jax.ShapeDtypeStruct((bsz, seq, wd), dt) for wd, dt in zip(widths, dtypes)],
        compiler_params=_cparams("arbitrary", "arbitrary"),
        name="norm_mod_inproj",
    )(h, nw.reshape(1, d), sc, sh, w)


def _unit_lower_inverses(mats, ri, ci):
    n = mats[0].shape[0]
    eye = (ri == ci).astype(F32)
    pair = (ri // 2 == ci // 2) & (ri > ci)
    ts = [eye - jnp.where(pair, a, 0.0) for a in mats]
    s = 2
    while s < n:
        sel = (ri // (2 * s) == ci // (2 * s)) & (ri // s > ci // s)
        tmps = [_mm3(t, jnp.where(sel, a, 0.0)) for t, a in zip(ts, mats)]
        ts = [t - _mm3(tmp, t) for t, tmp in zip(ts, tmps)]
        s *= 2
    return ts


def _gdn_kernel(x_ref, z_ref, ba_ref, cw_ref, garow_ref, gbrow_ref, hnw_ref, o_ref,
                xbuf_ref, state_ref, *, heads, tile):
    dh = GDN_HEAD_DIM
    width = heads * dh
    ck = GDN_CHUNK
    nck = tile // ck
    t_id = pl.program_id(1)

    @pl.when(t_id == 0)
    def _():
        xbuf_ref[0:SUBLANES, :] = jnp.zeros((SUBLANES, 3 * width), F32)
        state_ref[...] = jnp.zeros(state_ref.shape, F32)

    x = x_ref[0]
    xbuf_ref[SUBLANES:SUBLANES + tile, :] = x
    conv = cw_ref[0:1, :] * xbuf_ref[pl.ds(SUBLANES - CONV_K + 1, tile), :]
    for k in range(1, CONV_K):
        conv = conv + cw_ref[k:k + 1, :] * xbuf_ref[pl.ds(SUBLANES - CONV_K + 1 + k, tile), :]
    xbuf_ref[0:SUBLANES, :] = x[tile - SUBLANES:tile, :]
    qkv = _silu(conv)

    ba = ba_ref[0]
    beta_all = _sigmoid(ba)
    g_all = -jnp.exp(garow_ref[...]) * _softplus(ba + gbrow_ref[...])
    ri_t = lax.broadcasted_iota(jnp.int32, (tile, tile), 0)
    ci_t = lax.broadcasted_iota(jnp.int32, (tile, tile), 1)
    cum_mat = ((ri_t // ck == ci_t // ck) & (ri_t >= ci_t)).astype(BF16)
    g_hi = g_all.astype(BF16)
    g_r1 = g_all - g_hi.astype(F32)
    g_mid = g_r1.astype(BF16)
    g_lo = (g_r1 - g_mid.astype(F32)).astype(BF16)
    gc_all = _dot(cum_mat, g_hi) + (_dot(cum_mat, g_mid) + _dot(cum_mat, g_lo))
    gc_all_t = gc_all.T

    ri = lax.broadcasted_iota(jnp.int32, (ck, ck), 0)
    ci = lax.broadcasted_iota(jnp.int32, (ck, ck), 1)
    hnw = hnw_ref[...]
    z_all = z_ref[0]

    prep = []
    for n in range(nck):
        sl = slice(n * ck, (n + 1) * ck)
        for h in range(heads):
            q = qkv[sl, h * dh:(h + 1) * dh]
            k = qkv[sl, width + h * dh:width + (h + 1) * dh]
            v = qkv[sl, 2 * width + h * dh:2 * width + (h + 1) * dh]
            q = q * lax.rsqrt(jnp.sum(q * q, axis=-1, keepdims=True) + NORM_EPS) * (dh ** -0.5)
            k = k * lax.rsqrt(jnp.sum(k * k, axis=-1, keepdims=True) + NORM_EPS)
            beta = beta_all[sl, h:h + 1]
            gcc = gc_all[sl, heads + h:heads + h + 1]
            gcr = gc_all_t[heads + h:heads + h + 1, sl]
            g_last = gcc[ck - 1:ck, :]
            decay = jnp.where(ri >= ci, jnp.exp(jnp.minimum(gcc - gcr, 0.0)), 0.0)
            k_beta = k * beta
            kb16, k16 = k_beta.astype(BF16), k.astype(BF16)
            egc = jnp.exp(gcc)
            prep.append(dict(
                a=jnp.where(ri > ci, _dot_nt(kb16, k16) * decay, 0.0),
                rhs=jnp.concatenate([v * beta, k_beta * egc], axis=-1),
                attn=(_dot_nt(q.astype(BF16), k16) * decay).astype(BF16),
                q_dec=(q * egc).astype(BF16),
                k_dec=(k * jnp.exp(g_last - gcc)).astype(BF16),
                s_decay=jnp.exp(g_last)))
    invs = _unit_lower_inverses([p["a"] for p in prep], ri, ci)
    sols = [_mm3(t, p["rhs"]) for t, p in zip(invs, prep)]

    states = [state_ref[h] for h in range(heads)]
    outs = [[] for _ in range(heads)]
    for n in range(nck):
        cur = [(prep[n * heads + h], sols[n * heads + h]) for h in range(heads)]
        s16 = [st.astype(BF16) for st in states]
        vn16 = [(sol[:, :dh] - _dot(sol[:, dh:].astype(BF16), s)).astype(BF16) for (_, sol), s in zip(cur, s16)]
        for h in range(heads):
            p = cur[h][0]
            outs[h].append(_dot(p["q_dec"], s16[h]) + _dot(p["attn"], vn16[h]))
        states = [st * cur[h][0]["s_decay"] + _dot_tn(cur[h][0]["k_dec"], vn16[h]) for h, st in enumerate(states)]
    for h in range(heads):
        state_ref[h] = states[h]
        o_h = jnp.concatenate(outs[h], axis=0) if nck > 1 else outs[h][0]
        o_h = (_rms(o_h) * hnw) * _silu(z_all[:, h * dh:(h + 1) * dh])
        o_ref[0, :, h * dh:(h + 1) * dh] = o_h.astype(o_ref.dtype)


def _gdn(qkv_pre, z, ba, conv_w, a_log, dt_bias, head_norm_w):
    bsz, seq, w3 = qkv_pre.shape
    width = w3 // 3
    heads = width // GDN_HEAD_DIM
    tile = min(GDN_TILE, seq)
    garow = jnp.zeros((1, LANES), F32).at[0, heads:2 * heads].set(a_log.astype(F32))
    gbrow = jnp.zeros((1, LANES), F32).at[0, heads:2 * heads].set(dt_bias.astype(F32))
    row = lambda b, t: (b, t, 0)
    fix2 = lambda b, t: (0, 0)
    return pl.pallas_call(
        functools.partial(_gdn_kernel, heads=heads, tile=tile),
        grid=(bsz, seq // tile),
        in_specs=[pl.BlockSpec((1, tile, w3), row),
                  pl.BlockSpec((1, tile, width), row),
                  pl.BlockSpec((1, tile, LANES), row),
                  pl.BlockSpec((CONV_K, w3), fix2),
                  pl.BlockSpec((1, LANES), fix2),
                  pl.BlockSpec((1, LANES), fix2),
                  pl.BlockSpec((1, GDN_HEAD_DIM), fix2)],
        out_specs=pl.BlockSpec((1, tile, width), row),
        out_shape=jax.ShapeDtypeStruct((bsz, seq, width), BF16),
        scratch_shapes=[pltpu.VMEM((SUBLANES + tile, w3), F32),
                        pltpu.VMEM((heads, GDN_HEAD_DIM, GDN_HEAD_DIM), F32)],
        compiler_params=_cparams("arbitrary", "arbitrary"),
        name="gated_deltanet",
    )(qkv_pre, z, ba, conv_w.astype(F32), garow, gbrow, head_norm_w.astype(F32).reshape(1, -1))


def _s5_prep_kernel(lr_ref, li_ref, ldt_ref, btr_ref, bti_ref, cr_ref, ci_ref,
                    kst_ref, bjr_ref, bji_ref, cor_ref, coi_ref, l16r_ref, l16i_ref):
    nt = S5_CHUNK
    lr, li = lr_ref[0], li_ref[0]
    dt = jnp.exp(ldt_ref[0])
    mag = jnp.exp(lr * dt)
    lb_re, lb_im = mag * jnp.cos(li * dt), mag * jnp.sin(li * dt)
    den = lr * lr + li * li
    num_re, num_im = lb_re - 1.0, lb_im
    f_re = (num_re * lr + num_im * li) / den
    f_im = (num_im * lr - num_re * li) / den
    btr, bti = btr_ref[0], bti_ref[0]
    bb_re = f_re * btr - f_im * bti
    bb_im = f_re * bti + f_im * btr
    cr, ci = cr_ref[0], ci_ref[0]
    p_re, p_im = [jnp.ones_like(lr)], [jnp.zeros_like(lr)]
    for _ in range(nt):
        p_re.append(p_re[-1] * lb_re - p_im[-1] * lb_im)
        p_im.append(p_re[-2] * lb_im + p_im[-1] * lb_re)
    cp_re = [cr * pr - ci * pi for pr, pi in zip(p_re, p_im)]
    cp_im = [cr * pi + ci * pr for pr, pi in zip(p_re, p_im)]
    kst_ref[0] = (_mm3_nt(jnp.concatenate(cp_re[:nt], axis=0), bb_re)
                  - _mm3_nt(jnp.concatenate(cp_im[:nt], axis=0), bb_im))
    bjr_ref[0] = jnp.concatenate([bb_re * p_re[nt - 1 - s] - bb_im * p_im[nt - 1 - s] for s in range(nt)], axis=0)
    bji_ref[0] = jnp.concatenate([bb_re * p_im[nt - 1 - s] + bb_im * p_re[nt - 1 - s] for s in range(nt)], axis=0)
    cor_ref[0] = jnp.concatenate(cp_re[1:nt + 1], axis=0)
    coi_ref[0] = jnp.concatenate(cp_im[1:nt + 1], axis=0)
    l16r_ref[0] = p_re[nt]
    l16i_ref[0] = p_im[nt]


def _s5_prep(lam_re, lam_im, log_dt, b_re, b_im, c_re, c_im):
    g, p = lam_re.shape
    hs = c_re.shape[1]
    nt = S5_CHUNK
    vec = pl.BlockSpec((1, 1, p), lambda i: (i, 0, 0))
    mat = pl.BlockSpec((1, hs, p), lambda i: (i, 0, 0))
    big = pl.BlockSpec((1, nt * hs, p), lambda i: (i, 0, 0))
    f32 = lambda a: a.astype(F32)
    return pl.pallas_call(
        _s5_prep_kernel,
        grid=(g,),
        in_specs=[vec, vec, pl.BlockSpec((1, 1, 1), lambda i: (i, 0, 0)), mat, mat, mat, mat],
        out_specs=[pl.BlockSpec((1, nt * hs, hs), lambda i: (i, 0, 0)), big, big, big, big, vec, vec],
        out_shape=[jax.ShapeDtypeStruct((g, nt * hs, hs), F32)]
                  + [jax.ShapeDtypeStruct((g, nt * hs, p), F32)] * 4
                  + [jax.ShapeDtypeStruct((g, 1, p), F32)] * 2,
        compiler_params=_cparams("arbitrary"),
        name="s5_discretise",
    )(f32(lam_re).reshape(g, 1, p), f32(lam_im).reshape(g, 1, p), f32(log_dt).reshape(g, 1, 1),
      f32(b_re).transpose(0, 2, 1), f32(b_im).transpose(0, 2, 1), f32(c_re), f32(c_im))


def _s5_kernel(u_ref, toep_ref, binj_ref, ct_ref, dt_ref, l16r_ref, l16i_ref, y_ref,
               sre_ref, sim_ref, xre_ref, xim_ref, *, streams):
    rows = u_ref.shape[1]
    steps = rows // streams
    u = u_ref[0]
    ub = u.astype(BF16)
    inj = _dot(ub, binj_ref[0])
    sre_ref[...] = inj[:, :LANES]
    sim_ref[...] = inj[:, LANES:]
    lr = jnp.broadcast_to(l16r_ref[0], (streams, LANES))
    li = jnp.broadcast_to(l16i_ref[0], (streams, LANES))

    def scan_step(i, carry):
        xr, xi = carry
        r0 = pl.multiple_of(i * streams, streams)
        xre_ref[pl.ds(r0, streams), :] = xr
        xim_ref[pl.ds(r0, streams), :] = xi
        sr = sre_ref[pl.ds(r0, streams), :]
        si = sim_ref[pl.ds(r0, streams), :]
        return lr * xr - li * xi + sr, lr * xi + li * xr + si

    zero = jnp.zeros((streams, LANES), F32)
    xr_end, xi_end = lax.fori_loop(0, steps, scan_step, (zero, zero), unroll=8)

    odd = (lax.broadcasted_iota(jnp.int32, (streams, LANES), 0) % 2) == 1
    zr0 = jnp.where(odd, pltpu.roll(xr_end, 1, axis=0), 0.0)
    zi0 = jnp.where(odd, pltpu.roll(xi_end, 1, axis=0), 0.0)

    def carry_step(i, carry):
        zr, zi = carry
        r0 = pl.multiple_of(i * streams, streams)
        xre_ref[pl.ds(r0, streams), :] = xre_ref[pl.ds(r0, streams), :] + zr
        xim_ref[pl.ds(r0, streams), :] = xim_ref[pl.ds(r0, streams), :] + zi
        return lr * zr - li * zi, lr * zi + li * zr

    lax.fori_loop(0, steps, carry_step, (zr0, zi0), unroll=8)

    xin = jnp.concatenate([xre_ref[...], xim_ref[...]], axis=1).astype(BF16)
    y_ref[0] = _dot(ub, toep_ref[0]) + _dot(xin, ct_ref[0]) + dt_ref[0] * u


def _s5(u, lam_re, lam_im, log_dt, b_re, b_im, c_re, c_im, d_skip):
    bsz, seq, width = u.shape
    hs, p, nt = S5_GROUP, S5_STATE, S5_CHUNK
    g = width // hs
    streams = 2 * bsz
    assert streams == SUBLANES and p * 2 == LANES and nt * hs == 2 * LANES
    steps = seq // (2 * nt)
    rows = steps * streams
    kst, bjr, bji, cor, coi, l16r, l16i = _s5_prep(lam_re, lam_im, log_dt, b_re, b_im, c_re, c_im)

    s_idx = jnp.arange(nt)
    tau = s_idx[None, :] - s_idx[:, None]
    k4 = kst.reshape(g, nt, hs, hs)
    toep = jnp.where((tau >= 0)[None, :, :, None, None], k4[:, jnp.clip(tau, 0, nt - 1)], 0.0)
    toep = toep.transpose(0, 1, 4, 2, 3).reshape(g, nt * hs, nt * hs).astype(BF16)
    zpad = jnp.zeros((g, nt * hs, LANES - p), F32)
    binj = jnp.concatenate([bjr, zpad, bji, zpad], axis=-1).astype(BF16)
    zrow = jnp.zeros((g, LANES - p, nt * hs), F32)
    ct = jnp.concatenate([cor.transpose(0, 2, 1), zrow, -coi.transpose(0, 2, 1), zrow], axis=1).astype(BF16)
    dtile = jnp.tile(d_skip.astype(F32).reshape(g, 1, hs), (1, 1, nt))
    lpad = jnp.zeros((g, 1, LANES - p), F32)
    l16r = jnp.concatenate([l16r, lpad], axis=-1)
    l16i = jnp.concatenate([l16i, lpad], axis=-1)

    ut = u.reshape(bsz, 2, steps, nt, g, hs).transpose(4, 2, 0, 1, 3, 5).reshape(g, rows, nt * hs)
    per_g3 = lambda i: (i, 0, 0)
    y = pl.pallas_call(
        functools.partial(_s5_kernel, streams=streams),
        grid=(g,),
        in_specs=[pl.BlockSpec((1, rows, nt * hs), per_g3),
                  pl.BlockSpec((1, nt * hs, nt * hs), per_g3),
                  pl.BlockSpec((1, nt * hs, 2 * LANES), per_g3),
                  pl.BlockSpec((1, 2 * LANES, nt * hs), per_g3),
                  pl.BlockSpec((1, 1, nt * hs), per_g3),
                  pl.BlockSpec((1, 1, LANES), per_g3),
                  pl.BlockSpec((1, 1, LANES), per_g3)],
        out_specs=pl.BlockSpec((1, rows, nt * hs), per_g3),
        out_shape=jax.ShapeDtypeStruct((g, rows, nt * hs), F32),
        scratch_shapes=[pltpu.VMEM((rows, LANES), F32)] * 4,
        compiler_params=_cparams("arbitrary"),
        name="s5_chunked_scan",
    )(ut, toep, binj, ct, dtile, l16r, l16i)
    return y.reshape(g, steps, bsz, 2, nt, hs).transpose(2, 3, 1, 4, 0, 5).reshape(bsz, seq, width)


def _sb_kernel(q_ref, k_ref, v_ref, o_ref, acc_ref, run_ref, *, blk, nsub):
    qi = pl.program_id(2)
    ri = lax.broadcasted_iota(jnp.int32, (blk, blk), 0)
    ci = lax.broadcasted_iota(jnp.int32, (blk, blk), 1)
    after = (ri > ci).astype(BF16)
    valid = ci < ri

    def load_kv(kb):
        k0 = pl.multiple_of(kb * blk, blk)
        return k_ref[0, pl.ds(k0, blk), :], v_ref[0, pl.ds(k0, blk), :]

    def pre(sub, kblk, masked):
        s = _dot_nt(q_ref[0, sub * blk:(sub + 1) * blk, :], kblk)
        lse = jnp.log2(1.0 + jnp.exp2(-jnp.abs(s)))
        log_p = jnp.minimum(s, 0.0) - lse
        neg_log_1m = jnp.maximum(s, 0.0) + lse
        if masked:
            neg_log_1m = jnp.where(valid, neg_log_1m, 0.0)
        hi, lo = _split2(neg_log_1m)
        later = _dot(hi, after) + _dot(lo, after)
        return log_p - later, jnp.sum(neg_log_1m, axis=1, keepdims=True)

    def post(sub, part, tot, vblk, masked):
        rows = slice(sub * blk, (sub + 1) * blk)
        wts = jnp.exp2(part - run_ref[rows, :])
        if masked:
            wts = jnp.where(valid, wts, 0.0)
        acc_ref[rows, :] += _dot(wts.astype(BF16), vblk)
        run_ref[rows, :] += tot

    def sweep(kb, items):
        kblk, vblk = load_kv(kb)
        pres = [pre(sub, kblk, masked) for sub, masked in items]
        for (sub, masked), (part, tot) in zip(items, pres):
            post(sub, part, tot, vblk, masked)

    acc_ref[...] = jnp.zeros(acc_ref.shape, F32)
    run_ref[...] = jnp.zeros(run_ref.shape, F32)
    top = nsub * qi
    diag = []
    for j in reversed(range(nsub)):
        kblk, vblk = load_kv(top + j)
        diag.append((vblk, [(sub, sub == j, pre(sub, kblk, sub == j)) for sub in range(j, nsub)]))
    for vblk, items in diag:
        for sub, masked, (part, tot) in items:
            post(sub, part, tot, vblk, masked)

    def keep_going(state):
        i, run_min = state
        return jnp.logical_and(i < top, run_min < SB_UNDERFLOW_LOG2)

    def body(state):
        i, _ = state
        sweep(top - 1 - i, [(sub, False) for sub in range(nsub)])
        return i + 1, jnp.min(run_ref[...])

    lax.while_loop(keep_going, body, (jnp.int32(0), jnp.min(run_ref[...])))
    o_ref[0] = acc_ref[...].astype(o_ref.dtype)


def _sb_attention(q, kv, heads):
    bsz, seq, _ = q.shape
    d = SB_HEAD_DIM
    blk = min(SB_BLOCK, seq)
    nsub = max(1, min(SB_Q_TILE, seq) // blk)
    tq = nsub * blk
    return pl.pallas_call(
        functools.partial(_sb_kernel, blk=blk, nsub=nsub),
        grid=(bsz, heads, seq // tq),
        in_specs=[pl.BlockSpec((1, tq, d), lambda b, h, i: (b, i, h)),
                  pl.BlockSpec((1, seq, d), lambda b, h, i: (b, 0, h)),
                  pl.BlockSpec((1, seq, d), lambda b, h, i: (b, 0, heads + h))],
        out_specs=pl.BlockSpec((1, tq, d), lambda b, h, i: (b, i, h)),
        out_shape=jax.ShapeDtypeStruct((bsz, seq, heads * d), BF16),
        scratch_shapes=[pltpu.VMEM((tq, d), F32), pltpu.VMEM((tq, 1), F32)],
        compiler_params=_cparams("arbitrary", "arbitrary", "arbitrary"),
        name="stick_breaking_attention",
    )(q, kv, kv)


def _gelu_tanh(x):
    return 0.5 * x * (1.0 + jnp.tanh(0.7978845608028654 * (x + 0.044715 * (x * x * x))))


def _mix_ffn_kernel(*refs, even, final, f_chunks):
    if even:
        (h_ref, o_ref, y_ref, gw_ref, gb_ref, wtop_ref, wbot_ref, gtm_ref, nw_ref, sc_ref, sh_ref, gtf_ref,
         win_ref, wout_ref, fw_ref, out_ref) = refs
        y = _gelu_tanh(y_ref[0])
        y = y * _sigmoid(_dot(y.astype(BF16), gw_ref[...]) + gb_ref[...])
        mix = _dot(o_ref[0], wtop_ref[...]) + _dot(y.astype(BF16), wbot_ref[...])
    else:
        (h_ref, o_ref, wtop_ref, gtm_ref, nw_ref, sc_ref, sh_ref, gtf_ref,
         win_ref, wout_ref, fw_ref, out_ref) = refs
        mix = _dot(o_ref[0], wtop_ref[...])
    h1 = h_ref[0] + gtm_ref[0] * mix
    hn = _norm_mod(h1, nw_ref[...], sc_ref[0], sh_ref[0]).astype(BF16)
    hidden = wout_ref.shape[0]
    fc = hidden // f_chunks
    acc = None
    for c in range(f_chunks):
        gate = _dot(hn, win_ref[:, c * fc:(c + 1) * fc])
        up = _dot(hn, win_ref[:, hidden + c * fc:hidden + (c + 1) * fc])
        part = _dot((_silu(gate) * up).astype(BF16), wout_ref[c * fc:(c + 1) * fc, :])
        acc = part if acc is None else acc + part
    h2 = h1 + gtf_ref[0] * acc
    if final:
        h2 = _rms(h2) * fw_ref[...]
    out_ref[0] = h2


def _mix_ffn(h, acts, mix_w, glu, gt_m, nw, sc, sh, gt_f, w_in, w_out, final_w, *, even, final):
    bsz, seq, d = h.shape
    tm = min(ROW_TILE, seq)
    hidden = w_out.shape[0]
    row = lambda b, t: (b, t, 0)
    per_b = lambda b, t: (b, 0, 0)
    vec_d = pl.BlockSpec((1, d), lambda b, t: (0, 0))
    mod_spec = pl.BlockSpec((1, 1, d), per_b)
    args, specs = [h], [pl.BlockSpec((1, tm, d), row)]
    if even:
        o, y = acts
        half = o.shape[-1]
        glu_w, glu_b = glu
        args += [o, y, glu_w.astype(BF16), glu_b.astype(F32).reshape(1, -1),
                 mix_w[:half].astype(BF16), mix_w[half:].astype(BF16)]
        specs += [pl.BlockSpec((1, tm, half), row), pl.BlockSpec((1, tm, y.shape[-1]), row),
                  _const_spec(glu_w.shape), pl.BlockSpec((1, glu_b.shape[-1]), lambda b, t: (0, 0)),
                  _const_spec((half, d)), _const_spec((d - half, d))]
    else:
        (o,) = acts
        args += [o, mix_w.astype(BF16)]
        specs += [pl.BlockSpec((1, tm, d), row), _const_spec((d, d))]
    args += [gt_m, nw.astype(F32).reshape(1, d), sc, sh, gt_f, w_in.astype(BF16), w_out.astype(BF16),
             final_w.astype(F32).reshape(1, d)]
    specs += [mod_spec, vec_d, mod_spec, mod_spec, mod_spec, _const_spec(w_in.shape), _const_spec(w_out.shape), vec_d]
    return pl.pallas_call(
        functools.partial(_mix_ffn_kernel, even=even, final=final, f_chunks=2),
        grid=(bsz, seq // tm),
        in_specs=specs,
        out_specs=pl.BlockSpec((1, tm, d), row),
        out_shape=jax.ShapeDtypeStruct((bsz, seq, d), F32),
        compiler_params=_cparams("arbitrary", "arbitrary"),
        name="outproj_swiglu_ffn",
    )(*args)


def kernel(x, c, ada_w, ada_b, norm_mix_w, norm_ffn_w, ffn_w_in, ffn_w_out, hy_w_in, hy_conv_w, hy_a_log, hy_dt_bias, hy_head_norm_w, s5_lam_re, s5_lam_im, s5_log_dt, s5_b_re, s5_b_im, s5_c_re, s5_c_im, s5_d, s5_glu_w, s5_glu_b, hy_w_out, sb_w_in, sb_w_out, final_norm_w):
    bsz, seq, d = x.shape
    depth = ada_w.shape[0]
    gdn_width = d // 2
    gdn_heads = gdn_width // GDN_HEAD_DIM
    s5_width = d - gdn_width
    sb_heads = d // SB_HEAD_DIM

    mod = _modulation(c, ada_w, ada_b)
    h = x
    for i in range(depth):
        sh_m, sc_m, gt_m, sh_f, sc_f, gt_f = (mod[i, :bsz, k * d:(k + 1) * d].reshape(bsz, 1, d) for k in range(6))
        j = i // 2
        last = i == depth - 1
        if i % 2 == 0:
            w = hy_w_in[j]
            w4 = 4 * gdn_width
            gate_cols = jnp.pad(w[:, w4:w4 + 2 * gdn_heads], ((0, 0), (0, LANES - 2 * gdn_heads)))
            w_cat = jnp.concatenate([w[:, :w4], gate_cols, w[:, w4 + 2 * gdn_heads:]], axis=1).astype(BF16)
            qkv_pre, zg, ba, u = _inproj(h, norm_mix_w[i], sc_m, sh_m, w_cat,
                                         (3 * gdn_width, gdn_width, LANES, s5_width), (F32, F32, F32, F32))
            o = _gdn(qkv_pre, zg, ba, hy_conv_w[j], hy_a_log[j], hy_dt_bias[j], hy_head_norm_w[j])
            y = _s5(u, s5_lam_re[j], s5_lam_im[j], s5_log_dt[j], s5_b_re[j], s5_b_im[j],
                    s5_c_re[j], s5_c_im[j], s5_d[j])
            h = _mix_ffn(h, (o, y), hy_w_out[j], (s5_glu_w[j], s5_glu_b[j]), gt_m, norm_ffn_w[i], sc_f, sh_f, gt_f,
                         ffn_w_in[i], ffn_w_out[i], final_norm_w, even=True, final=last)
        else:
            q, kv = _inproj(h, norm_mix_w[i], sc_m, sh_m, sb_w_in[j].astype(BF16), (d, 2 * d), (BF16, BF16),
                            scales=(SB_HEAD_DIM ** -0.5 * LOG2_E, 1.0))
            o = _sb_attention(q, kv, sb_heads)
            h = _mix_ffn(h, (o,), sb_w_out[j], None, gt_m, norm_ffn_w[i], sc_f, sh_f, gt_f,
                         ffn_w_in[i], ffn_w_out[i], final_norm_w, even=False, final=last)
    return h
```

```python
import functools

import jax
import jax.numpy as jnp
import numpy as np
from jax import lax
from jax.experimental import pallas as pl
from jax.experimental.pallas import tpu as pltpu

F32 = jnp.float32
BF16 = jnp.bfloat16
NORM_EPS = 1e-6

GDN_HEAD_DIM = 128
CONV_K = 4
GDN_CHUNK = 64
S5_GROUP = 16
S5_STATE = 64
SB_HEAD_DIM = 128

LANES = 128
SUBLANES = 8
VMEM_LIMIT_BYTES = 56 * 1024 * 1024

ROW_TILE = 512
GDN_TILE = 256
S5_CHUNK = 16
SB_BLOCK = 256
SB_Q_TILE = 1024
LOG2_E = 1.4426950408889634
SB_UNDERFLOW_LOG2 = 160.0
MOD_COL_TILE = 1536


def _cparams(*sem):
    return pltpu.CompilerParams(dimension_semantics=sem, vmem_limit_bytes=VMEM_LIMIT_BYTES)


def _const_spec(shape):
    nd = len(shape)
    return pl.BlockSpec(shape, lambda *_: (0,) * nd, pipeline_mode=pl.Buffered(1))


def _dot(a, b):
    return jnp.dot(a, b, preferred_element_type=F32)


def _dot_nt(a, b):
    return lax.dot_general(a, b, (((1,), (1,)), ((), ())), preferred_element_type=F32)


def _dot_tn(a, b):
    return lax.dot_general(a, b, (((0,), (0,)), ((), ())), preferred_element_type=F32)


def _split2(x):
    hi = x.astype(BF16)
    lo = (x - hi.astype(F32)).astype(BF16)
    return hi, lo


def _mm3(a, b):
    ah, al = _split2(a)
    bh, bl = _split2(b)
    return _dot(ah, bh) + (_dot(ah, bl) + _dot(al, bh))


def _mm3_nt(a, b):
    ah, al = _split2(a)
    bh, bl = _split2(b)
    return _dot_nt(ah, bh) + (_dot_nt(ah, bl) + _dot_nt(al, bh))


def _sigmoid(x):
    return 1.0 / (1.0 + jnp.exp(-x))


def _silu(x):
    return x * _sigmoid(x)


def _softplus(x):
    return jnp.maximum(x, 0.0) + jnp.log(1.0 + jnp.exp(-jnp.abs(x)))


def _rms(x):
    return x * lax.rsqrt(jnp.mean(x * x, axis=-1, keepdims=True) + NORM_EPS)


def _norm_mod(x, nw, sc, sh):
    return (_rms(x) * nw) * (1.0 + sc) + sh


def _mod_kernel(c_ref, w_ref, b_ref, o_ref):
    ca = _silu(c_ref[...])
    o_ref[0] = _mm3(ca, w_ref[0]) + b_ref[0]


def _modulation(c, ada_w, ada_b):
    depth, d, n = ada_w.shape
    bsz = c.shape[0]
    rows = -(-bsz // SUBLANES) * SUBLANES
    c_pad = jnp.pad(c, ((0, rows - bsz), (0, 0)))
    tn = MOD_COL_TILE
    return pl.pallas_call(
        _mod_kernel,
        grid=(depth, n // tn),
        in_specs=[pl.BlockSpec((rows, d), lambda i, j: (0, 0)),
                  pl.BlockSpec((1, d, tn), lambda i, j: (i, 0, j)),
                  pl.BlockSpec((1, 1, tn), lambda i, j: (i, 0, j))],
        out_specs=pl.BlockSpec((1, rows, tn), lambda i, j: (i, 0, j)),
        out_shape=jax.ShapeDtypeStruct((depth, rows, n), F32),
        compiler_params=_cparams("arbitrary", "arbitrary"),
        name="adaln_modulation",
    )(c_pad, ada_w, ada_b.reshape(depth, 1, n))


def _s5_lane(group, s, hi=0):
    per_block = LANES // S5_GROUP
    return (s // per_block) * LANES + ((s + group) % per_block) * S5_GROUP + hi


def _to_chunk_major(x, z_ref, buf_ref):
    tokens, width = x.shape
    chunks = tokens // S5_CHUNK
    per_block = LANES // S5_GROUP
    for lb in range(width // LANES):
        buf_ref[lb] = x[:, lb * LANES:(lb + 1) * LANES]
    for s in range(S5_CHUNK):
        shift = (s % per_block) * S5_GROUP
        for lb in range(width // LANES):
            rows = buf_ref[lb, pl.ds(s, chunks, stride=S5_CHUNK), :]
            rows = pltpu.roll(rows, shift, axis=1) if shift else rows
            for j in range(per_block):
                src = ((s + j) % per_block) * S5_GROUP
                dst = _s5_lane(lb * per_block + j, s)
                z_ref[lb * per_block + j, :, dst:dst + S5_GROUP] = rows[:, src:src + S5_GROUP]


def _from_chunk_major(z_ref, buf_ref):
    groups, chunks, _ = z_ref.shape
    per_block = LANES // S5_GROUP
    nlb = groups // per_block
    lane_block = lax.broadcasted_iota(jnp.int32, (chunks, LANES), 1) // S5_GROUP
    for s in range(S5_CHUNK):
        half = s // per_block
        shift = (s % per_block) * S5_GROUP
        for lb in range(nlb):
            rows = None
            for j in range(per_block):
                piece = z_ref[lb * per_block + j, :, half * LANES:(half + 1) * LANES]
                rows = piece if rows is None else jnp.where(lane_block == (s + j) % per_block, piece, rows)
            rows = pltpu.roll(rows, LANES - shift, axis=1) if shift else rows
            buf_ref[lb, pl.ds(s, chunks, stride=S5_CHUNK), :] = rows
    return jnp.concatenate([buf_ref[lb] for lb in range(nlb)], axis=1)


def _inproj_kernel(h_ref, nw_ref, sc_ref, sh_ref, w_ref, *refs, widths, scales, chunk_major_last):
    out_refs = refs[:len(widths)]
    hn = _norm_mod(h_ref[0], nw_ref[...], sc_ref[0], sh_ref[0]).astype(BF16)
    off = 0
    for idx, (o_ref, wd, s) in enumerate(zip(out_refs, widths, scales)):
        acc = _dot(hn, w_ref[:, off:off + wd])
        if s != 1.0:
            acc = acc * s
        if chunk_major_last and idx == len(widths) - 1:
            _to_chunk_major(acc, o_ref, refs[len(widths)])
        else:
            o_ref[0] = acc.astype(o_ref.dtype)
        off += wd


def _inproj(h, nw, sc, sh, w, widths, dtypes, scales=None, chunk_major_last=False):
    scales = scales or (1.0,) * len(widths)
    bsz, seq, d = h.shape
    tm = min(ROW_TILE, seq)
    nt = seq // tm
    n = w.shape[1]
    row = lambda b, t: (b, t, 0)
    per_b = lambda b, t: (b, 0, 0)
    out_specs = [pl.BlockSpec((1, tm, wd), row) for wd in widths]
    out_shape = [jax.ShapeDtypeStruct((bsz, seq, wd), dt) for wd, dt in zip(widths, dtypes)]
    scratch = []
    if chunk_major_last:
        groups = widths[-1] // S5_GROUP
        row_w = S5_CHUNK * S5_GROUP
        out_specs[-1] = pl.BlockSpec((groups, tm // S5_CHUNK, row_w), lambda b, t: (0, b * nt + t, 0))
        out_shape[-1] = jax.ShapeDtypeStruct((groups, bsz * seq // S5_CHUNK, row_w), dtypes[-1])
        scratch = [pltpu.VMEM((widths[-1] // LANES, tm, LANES), F32)]
    return pl.pallas_call(
        functools.partial(_inproj_kernel, widths=widths, scales=scales, chunk_major_last=chunk_major_last),
        grid=(bsz, nt),
        in_specs=[pl.BlockSpec((1, tm, d), row),
                  pl.BlockSpec((1, d), lambda b, t: (0, 0)),
                  pl.BlockSpec((1, 1, d), per_b),
                  pl.BlockSpec((1, 1, d), per_b),
                  _const_spec((d, n))],
        out_specs=out_specs,
        out_shape=out_shape,
        scratch_shapes=scratch,
        compiler_params=_cparams("arbitrary", "arbitrary"),
        name="norm_mod_inproj",
    )(h, nw.reshape(1, d), sc, sh, w)


def _unit_lower_inverses(mats, ri, ci):
    n = mats[0].shape[0]
    eye = (ri == ci).astype(F32)
    pair = (ri // 2 == ci // 2) & (ri > ci)
    ts = [eye - jnp.where(pair, a, 0.0) for a in mats]
    s = 2
    while s < n:
        sel = (ri // (2 * s) == ci // (2 * s)) & (ri // s > ci // s)
        tmps = [_mm3(t, jnp.where(sel, a, 0.0)) for t, a in zip(ts, mats)]
        ts = [t - _mm3(tmp, t) for t, tmp in zip(ts, tmps)]
        s *= 2
    return ts


def _gdn_kernel(x_ref, z_ref, ba_ref, cw_ref, garow_ref, gbrow_ref, hnw_ref, o_ref,
                xbuf_ref, state_ref, *, heads, tile):
    dh = GDN_HEAD_DIM
    width = heads * dh
    ck = GDN_CHUNK
    nck = tile // ck
    t_id = pl.program_id(1)

    @pl.when(t_id == 0)
    def _():
        xbuf_ref[0:SUBLANES, :] = jnp.zeros((SUBLANES, 3 * width), F32)
        state_ref[...] = jnp.zeros(state_ref.shape, F32)

    x = x_ref[0]
    xbuf_ref[SUBLANES:SUBLANES + tile, :] = x
    conv = cw_ref[0:1, :] * xbuf_ref[pl.ds(SUBLANES - CONV_K + 1, tile), :]
    for k in range(1, CONV_K):
        conv = conv + cw_ref[k:k + 1, :] * xbuf_ref[pl.ds(SUBLANES - CONV_K + 1 + k, tile), :]
    xbuf_ref[0:SUBLANES, :] = x[tile - SUBLANES:tile, :]
    qkv = _silu(conv)

    ba = ba_ref[0]
    beta_all = _sigmoid(ba)
    g_all = -jnp.exp(garow_ref[...]) * _softplus(ba + gbrow_ref[...])
    ri_t = lax.broadcasted_iota(jnp.int32, (tile, tile), 0)
    ci_t = lax.broadcasted_iota(jnp.int32, (tile, tile), 1)
    cum_mat = ((ri_t // ck == ci_t // ck) & (ri_t >= ci_t)).astype(BF16)
    g_hi = g_all.astype(BF16)
    g_r1 = g_all - g_hi.astype(F32)
    g_mid = g_r1.astype(BF16)
    g_lo = (g_r1 - g_mid.astype(F32)).astype(BF16)
    gc_all = _dot(cum_mat, g_hi) + (_dot(cum_mat, g_mid) + _dot(cum_mat, g_lo))
    gc_all_t = gc_all.T

    ri = lax.broadcasted_iota(jnp.int32, (ck, ck), 0)
    ci = lax.broadcasted_iota(jnp.int32, (ck, ck), 1)
    hnw = hnw_ref[...]
    z_all = z_ref[0]

    prep = []
    for n in range(nck):
        sl = slice(n * ck, (n + 1) * ck)
        for h in range(heads):
            q = qkv[sl, h * dh:(h + 1) * dh]
            k = qkv[sl, width + h * dh:width + (h + 1) * dh]
            v = qkv[sl, 2 * width + h * dh:2 * width + (h + 1) * dh]
            q = q * lax.rsqrt(jnp.sum(q * q, axis=-1, keepdims=True) + NORM_EPS) * (dh ** -0.5)
            k = k * lax.rsqrt(jnp.sum(k * k, axis=-1, keepdims=True) + NORM_EPS)
            beta = beta_all[sl, h:h + 1]
            gcc = gc_all[sl, heads + h:heads + h + 1]
            gcr = gc_all_t[heads + h:heads + h + 1, sl]
            g_last = gcc[ck - 1:ck, :]
            decay = jnp.where(ri >= ci, jnp.exp(jnp.minimum(gcc - gcr, 0.0)), 0.0)
            k_beta = k * beta
            kb16, k16 = k_beta.astype(BF16), k.astype(BF16)
            egc = jnp.exp(gcc)
            prep.append(dict(
                a=jnp.where(ri > ci, _dot_nt(kb16, k16) * decay, 0.0),
                rhs=jnp.concatenate([v * beta, k_beta * egc], axis=-1),
                attn=(_dot_nt(q.astype(BF16), k16) * decay).astype(BF16),
                q_dec=(q * egc).astype(BF16),
                k_dec=(k * jnp.exp(g_last - gcc)).astype(BF16),
                s_decay=jnp.exp(g_last)))
    invs = _unit_lower_inverses([p["a"] for p in prep], ri, ci)
    sols = [_mm3(t, p["rhs"]) for t, p in zip(invs, prep)]

    states = [state_ref[h] for h in range(heads)]
    outs = [[] for _ in range(heads)]
    for n in range(nck):
        cur = [(prep[n * heads + h], sols[n * heads + h]) for h in range(heads)]
        s16 = [st.astype(BF16) for st in states]
        vn16 = [(sol[:, :dh] - _dot(sol[:, dh:].astype(BF16), s)).astype(BF16) for (_, sol), s in zip(cur, s16)]
        for h in range(heads):
            p = cur[h][0]
            outs[h].append(_dot(p["q_dec"], s16[h]) + _dot(p["attn"], vn16[h]))
        states = [st * cur[h][0]["s_decay"] + _dot_tn(cur[h][0]["k_dec"], vn16[h]) for h, st in enumerate(states)]
    for h in range(heads):
        state_ref[h] = states[h]
        o_h = jnp.concatenate(outs[h], axis=0) if nck > 1 else outs[h][0]
        o_h = (_rms(o_h) * hnw) * _silu(z_all[:, h * dh:(h + 1) * dh])
        o_ref[0, :, h * dh:(h + 1) * dh] = o_h.astype(o_ref.dtype)


def _gdn(qkv_pre, z, ba, conv_w, a_log, dt_bias, head_norm_w):
    bsz, seq, w3 = qkv_pre.shape
    width = w3 // 3
    heads = width // GDN_HEAD_DIM
    tile = min(GDN_TILE, seq)
    garow = jnp.zeros((1, LANES), F32).at[0, heads:2 * heads].set(a_log.astype(F32))
    gbrow = jnp.zeros((1, LANES), F32).at[0, heads:2 * heads].set(dt_bias.astype(F32))
    row = lambda b, t: (b, t, 0)
    fix2 = lambda b, t: (0, 0)
    return pl.pallas_call(
        functools.partial(_gdn_kernel, heads=heads, tile=tile),
        grid=(bsz, seq // tile),
        in_specs=[pl.BlockSpec((1, tile, w3), row),
                  pl.BlockSpec((1, tile, width), row),
                  pl.BlockSpec((1, tile, LANES), row),
                  pl.BlockSpec((CONV_K, w3), fix2),
                  pl.BlockSpec((1, LANES), fix2),
                  pl.BlockSpec((1, LANES), fix2),
                  pl.BlockSpec((1, GDN_HEAD_DIM), fix2)],
        out_specs=pl.BlockSpec((1, tile, width), row),
        out_shape=jax.ShapeDtypeStruct((bsz, seq, width), BF16),
        scratch_shapes=[pltpu.VMEM((SUBLANES + tile, w3), F32),
                        pltpu.VMEM((heads, GDN_HEAD_DIM, GDN_HEAD_DIM), F32)],
        compiler_params=_cparams("arbitrary", "arbitrary"),
        name="gated_deltanet",
    )(qkv_pre, z, ba, conv_w.astype(F32), garow, gbrow, head_norm_w.astype(F32).reshape(1, -1))


def _s5_prep_kernel(lr_ref, li_ref, ldt_ref, btr_ref, bti_ref, cr_ref, ci_ref,
                    kst_ref, bjr_ref, bji_ref, cor_ref, coi_ref, l16r_ref, l16i_ref):
    nt = S5_CHUNK
    lr, li = lr_ref[0], li_ref[0]
    dt = jnp.exp(ldt_ref[0])
    mag = jnp.exp(lr * dt)
    lb_re, lb_im = mag * jnp.cos(li * dt), mag * jnp.sin(li * dt)
    den = lr * lr + li * li
    num_re, num_im = lb_re - 1.0, lb_im
    f_re = (num_re * lr + num_im * li) / den
    f_im = (num_im * lr - num_re * li) / den
    btr, bti = btr_ref[0], bti_ref[0]
    bb_re = f_re * btr - f_im * bti
    bb_im = f_re * bti + f_im * btr
    cr, ci = cr_ref[0], ci_ref[0]
    p_re, p_im = [jnp.ones_like(lr)], [jnp.zeros_like(lr)]
    for _ in range(nt):
        p_re.append(p_re[-1] * lb_re - p_im[-1] * lb_im)
        p_im.append(p_re[-2] * lb_im + p_im[-1] * lb_re)
    cp_re = [cr * pr - ci * pi for pr, pi in zip(p_re, p_im)]
    cp_im = [cr * pi + ci * pr for pr, pi in zip(p_re, p_im)]
    kst_ref[0] = (_mm3_nt(jnp.concatenate(cp_re[:nt], axis=0), bb_re)
                  - _mm3_nt(jnp.concatenate(cp_im[:nt], axis=0), bb_im))
    bjr_ref[0] = jnp.concatenate([bb_re * p_re[nt - 1 - s] - bb_im * p_im[nt - 1 - s] for s in range(nt)], axis=0)
    bji_ref[0] = jnp.concatenate([bb_re * p_im[nt - 1 - s] + bb_im * p_re[nt - 1 - s] for s in range(nt)], axis=0)
    cor_ref[0] = jnp.concatenate(cp_re[1:nt + 1], axis=0)
    coi_ref[0] = jnp.concatenate(cp_im[1:nt + 1], axis=0)
    l16r_ref[0] = p_re[nt]
    l16i_ref[0] = p_im[nt]


def _s5_prep(lam_re, lam_im, log_dt, b_re, b_im, c_re, c_im):
    g, p = lam_re.shape
    hs = c_re.shape[1]
    nt = S5_CHUNK
    vec = pl.BlockSpec((1, 1, p), lambda i: (i, 0, 0))
    mat = pl.BlockSpec((1, hs, p), lambda i: (i, 0, 0))
    big = pl.BlockSpec((1, nt * hs, p), lambda i: (i, 0, 0))
    f32 = lambda a: a.astype(F32)
    return pl.pallas_call(
        _s5_prep_kernel,
        grid=(g,),
        in_specs=[vec, vec, pl.BlockSpec((1, 1, 1), lambda i: (i, 0, 0)), mat, mat, mat, mat],
        out_specs=[pl.BlockSpec((1, nt * hs, hs), lambda i: (i, 0, 0)), big, big, big, big, vec, vec],
        out_shape=[jax.ShapeDtypeStruct((g, nt * hs, hs), F32)]
                  + [jax.ShapeDtypeStruct((g, nt * hs, p), F32)] * 4
                  + [jax.ShapeDtypeStruct((g, 1, p), F32)] * 2,
        compiler_params=_cparams("arbitrary"),
        name="s5_discretise",
    )(f32(lam_re).reshape(g, 1, p), f32(lam_im).reshape(g, 1, p), f32(log_dt).reshape(g, 1, 1),
      f32(b_re).transpose(0, 2, 1), f32(b_im).transpose(0, 2, 1), f32(c_re), f32(c_im))


def _s5_kernel(u_ref, toep_ref, binj_ref, ct_ref, dt_ref, l16r_ref, l16i_ref, y_ref,
               sre_ref, sim_ref, xre_ref, xim_ref, *, streams):
    rows = u_ref.shape[1]
    steps = rows // streams
    u = u_ref[0]
    ub = u.astype(BF16)
    inj = _dot(ub, binj_ref[0])
    sre_ref[...] = inj[:, :LANES]
    sim_ref[...] = inj[:, LANES:]
    lr = jnp.broadcast_to(l16r_ref[0], (streams, LANES))
    li = jnp.broadcast_to(l16i_ref[0], (streams, LANES))

    def scan_step(i, carry):
        xr, xi = carry
        at = pl.ds(i, streams, stride=steps)
        xre_ref[at, :] = xr
        xim_ref[at, :] = xi
        sr = sre_ref[at, :]
        si = sim_ref[at, :]
        return lr * xr - li * xi + sr, lr * xi + li * xr + si

    zero = jnp.zeros((streams, LANES), F32)
    lax.fori_loop(0, steps, scan_step, (zero, zero), unroll=8)

    xin = jnp.concatenate([xre_ref[...], xim_ref[...]], axis=1).astype(BF16)
    y_ref[0] = _dot(ub, toep_ref[0]) + _dot(xin, ct_ref[0]) + dt_ref[0] * u


def _s5(ut, bsz, lam_re, lam_im, log_dt, b_re, b_im, c_re, c_im, d_skip):
    g, rows, _ = ut.shape
    hs, p, nt = S5_GROUP, S5_STATE, S5_CHUNK
    streams = bsz
    assert p * 2 == LANES and nt * hs == 2 * LANES
    steps = rows // streams
    kst, bjr, bji, cor, coi, l16r, l16i = _s5_prep(lam_re, lam_im, log_dt, b_re, b_im, c_re, c_im)

    s_idx = jnp.arange(nt)
    tau = s_idx[None, :] - s_idx[:, None]
    k4 = kst.reshape(g, nt, hs, hs)
    toep = jnp.where((tau >= 0)[None, :, :, None, None], k4[:, jnp.clip(tau, 0, nt - 1)], 0.0)
    toep = toep.transpose(0, 1, 4, 2, 3).reshape(g, nt * hs, nt * hs).astype(BF16)
    zpad = jnp.zeros((g, nt * hs, LANES - p), F32)
    binj = jnp.concatenate([bjr, zpad, bji, zpad], axis=-1).astype(BF16)
    zrow = jnp.zeros((g, LANES - p, nt * hs), F32)
    ct = jnp.concatenate([cor.transpose(0, 2, 1), zrow, -coi.transpose(0, 2, 1), zrow], axis=1).astype(BF16)
    dtile = jnp.tile(d_skip.astype(F32).reshape(g, 1, hs), (1, 1, nt))
    lpad = jnp.zeros((g, 1, LANES - p), F32)
    l16r = jnp.concatenate([l16r, lpad], axis=-1)
    l16i = jnp.concatenate([l16i, lpad], axis=-1)
    lane_of = np.array([[_s5_lane(gg, s, hi) for s in range(nt) for hi in range(hs)] for gg in range(g)])
    src_of_lane = jnp.asarray(np.argsort(lane_of, axis=1).astype(np.int32))
    toep = jnp.take_along_axis(jnp.take_along_axis(toep, src_of_lane[:, :, None], axis=1),
                               src_of_lane[:, None, :], axis=2)
    binj = jnp.take_along_axis(binj, src_of_lane[:, :, None], axis=1)
    ct = jnp.take_along_axis(ct, src_of_lane[:, None, :], axis=2)
    dtile = jnp.take_along_axis(dtile, src_of_lane[:, None, :], axis=2)

    per_g3 = lambda i: (i, 0, 0)
    return pl.pallas_call(
        functools.partial(_s5_kernel, streams=streams),
        grid=(g,),
        in_specs=[pl.BlockSpec((1, rows, nt * hs), per_g3),
                  pl.BlockSpec((1, nt * hs, nt * hs), per_g3),
                  pl.BlockSpec((1, nt * hs, 2 * LANES), per_g3),
                  pl.BlockSpec((1, 2 * LANES, nt * hs), per_g3),
                  pl.BlockSpec((1, 1, nt * hs), per_g3),
                  pl.BlockSpec((1, 1, LANES), per_g3),
                  pl.BlockSpec((1, 1, LANES), per_g3)],
        out_specs=pl.BlockSpec((1, rows, nt * hs), per_g3),
        out_shape=jax.ShapeDtypeStruct((g, rows, nt * hs), F32),
        scratch_shapes=[pltpu.VMEM((rows, LANES), F32)] * 4,
        compiler_params=_cparams("arbitrary"),
        name="s5_chunked_scan",
    )(ut, toep, binj, ct, dtile, l16r, l16i)


def _sb_kernel(q_ref, k_ref, v_ref, o_ref, acc_ref, run_ref, *, blk, nsub):
    qi = pl.program_id(2)
    ri = lax.broadcasted_iota(jnp.int32, (blk, blk), 0)
    ci = lax.broadcasted_iota(jnp.int32, (blk, blk), 1)
    after = (ri > ci).astype(BF16)
    valid = ci < ri

    def load_kv(kb):
        k0 = pl.multiple_of(kb * blk, blk)
        return k_ref[0, pl.ds(k0, blk), :], v_ref[0, pl.ds(k0, blk), :]

    def pre(sub, kblk, masked):
        s = _dot_nt(q_ref[0, sub * blk:(sub + 1) * blk, :], kblk)
        lse = jnp.log2(1.0 + jnp.exp2(-jnp.abs(s)))
        log_p = jnp.minimum(s, 0.0) - lse
        neg_log_1m = jnp.maximum(s, 0.0) + lse
        if masked:
            neg_log_1m = jnp.where(valid, neg_log_1m, 0.0)
        hi, lo = _split2(neg_log_1m)
        later = _dot(hi, after) + _dot(lo, after)
        return log_p - later, jnp.sum(neg_log_1m, axis=1, keepdims=True)

    def post(sub, part, tot, vblk, masked):
        rows = slice(sub * blk, (sub + 1) * blk)
        wts = jnp.exp2(part - run_ref[rows, :])
        if masked:
            wts = jnp.where(valid, wts, 0.0)
        acc_ref[rows, :] += _dot(wts.astype(BF16), vblk)
        run_ref[rows, :] += tot

    def sweep(kb, items):
        kblk, vblk = load_kv(kb)
        pres = [pre(sub, kblk, masked) for sub, masked in items]
        for (sub, masked), (part, tot) in zip(items, pres):
            post(sub, part, tot, vblk, masked)

    acc_ref[...] = jnp.zeros(acc_ref.shape, F32)
    run_ref[...] = jnp.zeros(run_ref.shape, F32)
    top = nsub * qi
    diag = []
    for j in reversed(range(nsub)):
        kblk, vblk = load_kv(top + j)
        diag.append((vblk, [(sub, sub == j, pre(sub, kblk, sub == j)) for sub in range(j, nsub)]))
    for vblk, items in diag:
        for sub, masked, (part, tot) in items:
            post(sub, part, tot, vblk, masked)

    def keep_going(state):
        i, run_min = state
        return jnp.logical_and(i < top, run_min < SB_UNDERFLOW_LOG2)

    def body(state):
        i, _ = state
        sweep(top - 1 - i, [(sub, False) for sub in range(nsub)])
        return i + 1, jnp.min(run_ref[...])

    lax.while_loop(keep_going, body, (jnp.int32(0), jnp.min(run_ref[...])))
    o_ref[0] = acc_ref[...].astype(o_ref.dtype)


def _sb_attention(q, kv, heads):
    bsz, seq, _ = q.shape
    d = SB_HEAD_DIM
    blk = min(SB_BLOCK, seq)
    nsub = max(1, min(SB_Q_TILE, seq) // blk)
    tq = nsub * blk
    return pl.pallas_call(
        functools.partial(_sb_kernel, blk=blk, nsub=nsub),
        grid=(bsz, heads, seq // tq),
        in_specs=[pl.BlockSpec((1, tq, d), lambda b, h, i: (b, i, h)),
                  pl.BlockSpec((1, seq, d), lambda b, h, i: (b, 0, h)),
                  pl.BlockSpec((1, seq, d), lambda b, h, i: (b, 0, heads + h))],
        out_specs=pl.BlockSpec((1, tq, d), lambda b, h, i: (b, i, h)),
        out_shape=jax.ShapeDtypeStruct((bsz, seq, heads * d), BF16),
        scratch_shapes=[pltpu.VMEM((tq, d), F32), pltpu.VMEM((tq, 1), F32)],
        compiler_params=_cparams("arbitrary", "arbitrary", "arbitrary"),
        name="stick_breaking_attention",
    )(q, kv, kv)


def _gelu_tanh(x):
    return 0.5 * x * (1.0 + jnp.tanh(0.7978845608028654 * (x + 0.044715 * (x * x * x))))


def _mix_ffn_kernel(*refs, even, final, f_chunks):
    if even:
        (h_ref, o_ref, y_ref, gw_ref, gb_ref, wtop_ref, wbot_ref, gtm_ref, nw_ref, sc_ref, sh_ref, gtf_ref,
         win_ref, wout_ref, fw_ref, out_ref, ybuf_ref) = refs
        y = _gelu_tanh(_from_chunk_major(y_ref, ybuf_ref))
        y = y * _sigmoid(_dot(y.astype(BF16), gw_ref[...]) + gb_ref[...])
        mix = _dot(o_ref[0], wtop_ref[...]) + _dot(y.astype(BF16), wbot_ref[...])
    else:
        (h_ref, o_ref, wtop_ref, gtm_ref, nw_ref, sc_ref, sh_ref, gtf_ref,
         win_ref, wout_ref, fw_ref, out_ref) = refs
        mix = _dot(o_ref[0], wtop_ref[...])
    h1 = h_ref[0] + gtm_ref[0] * mix
    hn = _norm_mod(h1, nw_ref[...], sc_ref[0], sh_ref[0]).astype(BF16)
    hidden = wout_ref.shape[0]
    fc = hidden // f_chunks
    acc = None
    for c in range(f_chunks):
        gate = _dot(hn, win_ref[:, c * fc:(c + 1) * fc])
        up = _dot(hn, win_ref[:, hidden + c * fc:hidden + (c + 1) * fc])
        part = _dot((_silu(gate) * up).astype(BF16), wout_ref[c * fc:(c + 1) * fc, :])
        acc = part if acc is None else acc + part
    h2 = h1 + gtf_ref[0] * acc
    if final:
        h2 = _rms(h2) * fw_ref[...]
    out_ref[0] = h2


def _mix_ffn(h, acts, mix_w, glu, gt_m, nw, sc, sh, gt_f, w_in, w_out, final_w, *, even, final):
    bsz, seq, d = h.shape
    tm = min(ROW_TILE, seq)
    hidden = w_out.shape[0]
    row = lambda b, t: (b, t, 0)
    per_b = lambda b, t: (b, 0, 0)
    vec_d = pl.BlockSpec((1, d), lambda b, t: (0, 0))
    mod_spec = pl.BlockSpec((1, 1, d), per_b)
    args, specs, scratch = [h], [pl.BlockSpec((1, tm, d), row)], []
    if even:
        o, y = acts
        half = o.shape[-1]
        nt = seq // tm
        glu_w, glu_b = glu
        args += [o, y, glu_w.astype(BF16), glu_b.astype(F32).reshape(1, -1),
                 mix_w[:half].astype(BF16), mix_w[half:].astype(BF16)]
        specs += [pl.BlockSpec((1, tm, half), row),
                  pl.BlockSpec((y.shape[0], tm // S5_CHUNK, y.shape[2]), lambda b, t: (0, b * nt + t, 0)),
                  _const_spec(glu_w.shape), pl.BlockSpec((1, glu_b.shape[-1]), lambda b, t: (0, 0)),
                  _const_spec((half, d)), _const_spec((d - half, d))]
        scratch = [pltpu.VMEM(((d - half) // LANES, tm, LANES), F32)]
    else:
        (o,) = acts
        args += [o, mix_w.astype(BF16)]
        specs += [pl.BlockSpec((1, tm, d), row), _const_spec((d, d))]
    args += [gt_m, nw.astype(F32).reshape(1, d), sc, sh, gt_f, w_in.astype(BF16), w_out.astype(BF16),
             final_w.astype(F32).reshape(1, d)]
    specs += [mod_spec, vec_d, mod_spec, mod_spec, mod_spec, _const_spec(w_in.shape), _const_spec(w_out.shape), vec_d]
    return pl.pallas_call(
        functools.partial(_mix_ffn_kernel, even=even, final=final, f_chunks=2),
        grid=(bsz, seq // tm),
        in_specs=specs,
        out_specs=pl.BlockSpec((1, tm, d), row),
        out_shape=jax.ShapeDtypeStruct((bsz, seq, d), F32),
        scratch_shapes=scratch,
        compiler_params=_cparams("arbitrary", "arbitrary"),
        name="outproj_swiglu_ffn",
    )(*args)


def kernel(x, c, ada_w, ada_b, norm_mix_w, norm_ffn_w, ffn_w_in, ffn_w_out, hy_w_in, hy_conv_w, hy_a_log, hy_dt_bias, hy_head_norm_w, s5_lam_re, s5_lam_im, s5_log_dt, s5_b_re, s5_b_im, s5_c_re, s5_c_im, s5_d, s5_glu_w, s5_glu_b, hy_w_out, sb_w_in, sb_w_out, final_norm_w):
    bsz, seq, d = x.shape
    depth = ada_w.shape[0]
    gdn_width = d // 2
    gdn_heads = gdn_width // GDN_HEAD_DIM
    s5_width = d - gdn_width
    sb_heads = d // SB_HEAD_DIM

    mod = _modulation(c, ada_w, ada_b)
    h = x
    for i in range(depth):
        sh_m, sc_m, gt_m, sh_f, sc_f, gt_f = (mod[i, :bsz, k * d:(k + 1) * d].reshape(bsz, 1, d) for k in range(6))
        j = i // 2
        last = i == depth - 1
        if i % 2 == 0:
            w = hy_w_in[j]
            w4 = 4 * gdn_width
            gate_cols = jnp.pad(w[:, w4:w4 + 2 * gdn_heads], ((0, 0), (0, LANES - 2 * gdn_heads)))
            w_cat = jnp.concatenate([w[:, :w4], gate_cols, w[:, w4 + 2 * gdn_heads:]], axis=1).astype(BF16)
            qkv_pre, zg, ba, ut = _inproj(h, norm_mix_w[i], sc_m, sh_m, w_cat,
                                          (3 * gdn_width, gdn_width, LANES, s5_width), (F32, F32, F32, F32),
                                          chunk_major_last=True)
            o = _gdn(qkv_pre, zg, ba, hy_conv_w[j], hy_a_log[j], hy_dt_bias[j], hy_head_norm_w[j])
            y = _s5(ut, bsz, s5_lam_re[j], s5_lam_im[j], s5_log_dt[j], s5_b_re[j], s5_b_im[j],
                    s5_c_re[j], s5_c_im[j], s5_d[j])
            h = _mix_ffn(h, (o, y), hy_w_out[j], (s5_glu_w[j], s5_glu_b[j]), gt_m, norm_ffn_w[i], sc_f, sh_f, gt_f,
                         ffn_w_in[i], ffn_w_out[i], final_norm_w, even=True, final=last)
        else:
            q, kv = _inproj(h, norm_mix_w[i], sc_m, sh_m, sb_w_in[j].astype(BF16), (d, 2 * d), (BF16, BF16),
                            scales=(SB_HEAD_DIM ** -0.5 * LOG2_E, 1.0))
            o = _sb_attention(q, kv, sb_heads)
            h = _mix_ffn(h, (o,), sb_w_out[j], None, gt_m, norm_ffn_w[i], sc_f, sh_f, gt_f,
                         ffn_w_in[i], ffn_w_out[i], final_norm_w, even=False, final=last)
    return h
```

```python
import functools

import jax
import jax.numpy as jnp
import numpy as np
from jax import lax
from jax.experimental import pallas as pl
from jax.experimental.pallas import tpu as pltpu

F32 = jnp.float32
BF16 = jnp.bfloat16
NORM_EPS = 1e-6

GDN_HEAD_DIM = 128
CONV_K = 4
GDN_CHUNK = 64
S5_GROUP = 16
S5_STATE = 64
SB_HEAD_DIM = 128

LANES = 128
SUBLANES = 8
VMEM_LIMIT_BYTES = 56 * 1024 * 1024

ROW_TILE = 512
GDN_TILE = 256
S5_CHUNK = 16
SB_KEY_BLOCK = 256
SB_QUERY_BLOCK = 128
SB_Q_TILE = 2048
SB_STATIC_BANDS = 2
LOG2_E = 1.4426950408889634
SB_UNDERFLOW_LOG2 = 160.0
MOD_COL_TILE = 1536


def _cparams(*sem):
    return pltpu.CompilerParams(dimension_semantics=sem, vmem_limit_bytes=VMEM_LIMIT_BYTES)


def _const_spec(shape):
    nd = len(shape)
    return pl.BlockSpec(shape, lambda *_: (0,) * nd, pipeline_mode=pl.Buffered(1))


def _dot(a, b):
    return jnp.dot(a, b, preferred_element_type=F32)


def _dot_nt(a, b):
    return lax.dot_general(a, b, (((1,), (1,)), ((), ())), preferred_element_type=F32)


def _dot_tn(a, b):
    return lax.dot_general(a, b, (((0,), (0,)), ((), ())), preferred_element_type=F32)


def _split2(x):
    hi = x.astype(BF16)
    lo = (x - hi.astype(F32)).astype(BF16)
    return hi, lo


def _mm3(a, b):
    ah, al = _split2(a)
    bh, bl = _split2(b)
    return _dot(ah, bh) + (_dot(ah, bl) + _dot(al, bh))


def _mm3_nt(a, b):
    ah, al = _split2(a)
    bh, bl = _split2(b)
    return _dot_nt(ah, bh) + (_dot_nt(ah, bl) + _dot_nt(al, bh))


def _sigmoid(x):
    return 1.0 / (1.0 + jnp.exp(-x))


def _silu(x):
    return x * _sigmoid(x)


def _softplus(x):
    return jnp.maximum(x, 0.0) + jnp.log(1.0 + jnp.exp(-jnp.abs(x)))


def _rms(x):
    return x * lax.rsqrt(jnp.mean(x * x, axis=-1, keepdims=True) + NORM_EPS)


def _norm_mod(x, nw, sc, sh):
    return (_rms(x) * nw) * (1.0 + sc) + sh


def _mod_kernel(c_ref, w_ref, b_ref, o_ref):
    ca = _silu(c_ref[...])
    o_ref[0] = _mm3(ca, w_ref[0]) + b_ref[0]


def _modulation(c, ada_w, ada_b):
    depth, d, n = ada_w.shape
    bsz = c.shape[0]
    rows = -(-bsz // SUBLANES) * SUBLANES
    c_pad = jnp.pad(c, ((0, rows - bsz), (0, 0)))
    tn = MOD_COL_TILE
    return pl.pallas_call(
        _mod_kernel,
        grid=(depth, n // tn),
        in_specs=[pl.BlockSpec((rows, d), lambda i, j: (0, 0)),
                  pl.BlockSpec((1, d, tn), lambda i, j: (i, 0, j)),
                  pl.BlockSpec((1, 1, tn), lambda i, j: (i, 0, j))],
        out_specs=pl.BlockSpec((1, rows, tn), lambda i, j: (i, 0, j)),
        out_shape=jax.ShapeDtypeStruct((depth, rows, n), F32),
        compiler_params=_cparams("arbitrary", "arbitrary"),
        name="adaln_modulation",
    )(c_pad, ada_w, ada_b.reshape(depth, 1, n))


def _s5_lane(group, s, hi=0):
    per_block = LANES // S5_GROUP
    return (s // per_block) * LANES + ((s + group) % per_block) * S5_GROUP + hi


def _to_lane_order(a, axis):
    per_block = LANES // S5_GROUP
    shp = a.shape
    a = a.reshape((shp[0] // per_block, per_block) + shp[1:axis]
                  + (S5_CHUNK // per_block, per_block, S5_GROUP) + shp[axis + 1:])
    parts = [jnp.roll(a[:, j], j, axis=axis + 1) for j in range(per_block)]
    return jnp.stack(parts, axis=1).reshape(shp)


def _to_chunk_major(x, z_ref, buf_ref):
    tokens, width = x.shape
    chunks = tokens // S5_CHUNK
    per_block = LANES // S5_GROUP
    for lb in range(width // LANES):
        buf_ref[lb] = x[:, lb * LANES:(lb + 1) * LANES]
    for s in range(S5_CHUNK):
        shift = (s % per_block) * S5_GROUP
        for lb in range(width // LANES):
            rows = buf_ref[lb, pl.ds(s, chunks, stride=S5_CHUNK), :]
            rows = pltpu.roll(rows, shift, axis=1) if shift else rows
            for j in range(per_block):
                src = ((s + j) % per_block) * S5_GROUP
                dst = _s5_lane(lb * per_block + j, s)
                z_ref[lb * per_block + j, :, dst:dst + S5_GROUP] = rows[:, src:src + S5_GROUP]


def _from_chunk_major(z_ref, buf_ref):
    groups, chunks, _ = z_ref.shape
    per_block = LANES // S5_GROUP
    nlb = groups // per_block
    lane_block = lax.broadcasted_iota(jnp.int32, (chunks, LANES), 1) // S5_GROUP
    for s in range(S5_CHUNK):
        half = s // per_block
        shift = (s % per_block) * S5_GROUP
        for lb in range(nlb):
            rows = None
            for j in range(per_block):
                piece = z_ref[lb * per_block + j, :, half * LANES:(half + 1) * LANES]
                rows = piece if rows is None else jnp.where(lane_block == (s + j) % per_block, piece, rows)
            rows = pltpu.roll(rows, LANES - shift, axis=1) if shift else rows
            buf_ref[lb, pl.ds(s, chunks, stride=S5_CHUNK), :] = rows
    return jnp.concatenate([buf_ref[lb] for lb in range(nlb)], axis=1)


def _inproj_kernel(h_ref, nw_ref, sc_ref, sh_ref, w_ref, *refs, widths, scales, chunk_major_last):
    out_refs = refs[:len(widths)]
    hn = _norm_mod(h_ref[0], nw_ref[...], sc_ref[0], sh_ref[0]).astype(BF16)
    off = 0
    for idx, (o_ref, wd, s) in enumerate(zip(out_refs, widths, scales)):
        acc = _dot(hn, w_ref[:, off:off + wd])
        if s != 1.0:
            acc = acc * s
        if chunk_major_last and idx == len(widths) - 1:
            _to_chunk_major(acc, o_ref, refs[len(widths)])
        else:
            o_ref[0] = acc.astype(o_ref.dtype)
        off += wd


def _inproj(h, nw, sc, sh, w, widths, dtypes, scales=None, chunk_major_last=False):
    scales = scales or (1.0,) * len(widths)
    bsz, seq, d = h.shape
    tm = min(ROW_TILE, seq)
    nt = seq // tm
    n = w.shape[1]
    row = lambda b, t: (b, t, 0)
    per_b = lambda b, t: (b, 0, 0)
    out_specs = [pl.BlockSpec((1, tm, wd), row) for wd in widths]
    out_shape = [jax.ShapeDtypeStruct((bsz, seq, wd), dt) for wd, dt in zip(widths, dtypes)]
    scratch = []
    if chunk_major_last:
        groups = widths[-1] // S5_GROUP
        row_w = S5_CHUNK * S5_GROUP
        out_specs[-1] = pl.BlockSpec((groups, tm // S5_CHUNK, row_w), lambda b, t: (0, b * nt + t, 0))
        out_shape[-1] = jax.ShapeDtypeStruct((groups, bsz * seq // S5_CHUNK, row_w), dtypes[-1])
        scratch = [pltpu.VMEM((widths[-1] // LANES, tm, LANES), F32)]
    return pl.pallas_call(
        functools.partial(_inproj_kernel, widths=widths, scales=scales, chunk_major_last=chunk_major_last),
        grid=(bsz, nt),
        in_specs=[pl.BlockSpec((1, tm, d), row),
                  pl.BlockSpec((1, d), lambda b, t: (0, 0)),
                  pl.BlockSpec((1, 1, d), per_b),
                  pl.BlockSpec((1, 1, d), per_b),
                  _const_spec((d, n))],
        out_specs=out_specs,
        out_shape=out_shape,
        scratch_shapes=scratch,
        compiler_params=_cparams("arbitrary", "arbitrary"),
        name="norm_mod_inproj",
    )(h, nw.reshape(1, d), sc, sh, w)


def _unit_lower_inverses(mats, ri, ci):
    n = mats[0].shape[0]
    eye = (ri == ci).astype(F32)
    pair = (ri // 2 == ci // 2) & (ri > ci)
    ts = [eye - jnp.where(pair, a, 0.0) for a in mats]
    s = 2
    while s < n:
        sel = (ri // (2 * s) == ci // (2 * s)) & (ri // s > ci // s)
        tmps = [_mm3(t, jnp.where(sel, a, 0.0)) for t, a in zip(ts, mats)]
        ts = [t - _mm3(tmp, t) for t, tmp in zip(ts, tmps)]
        s *= 2
    return ts


def _gdn_kernel(x_ref, z_ref, ba_ref, cw_ref, garow_ref, gbrow_ref, hnw_ref, o_ref,
                xbuf_ref, state_ref, *, heads, tile):
    dh = GDN_HEAD_DIM
    width = heads * dh
    ck = GDN_CHUNK
    nck = tile // ck
    t_id = pl.program_id(1)

    @pl.when(t_id == 0)
    def _():
        xbuf_ref[0:SUBLANES, :] = jnp.zeros((SUBLANES, 3 * width), F32)
        state_ref[...] = jnp.zeros(state_ref.shape, F32)

    x = x_ref[0]
    xbuf_ref[SUBLANES:SUBLANES + tile, :] = x
    conv = cw_ref[0:1, :] * xbuf_ref[pl.ds(SUBLANES - CONV_K + 1, tile), :]
    for k in range(1, CONV_K):
        conv = conv + cw_ref[k:k + 1, :] * xbuf_ref[pl.ds(SUBLANES - CONV_K + 1 + k, tile), :]
    xbuf_ref[0:SUBLANES, :] = x[tile - SUBLANES:tile, :]
    qkv = _silu(conv)

    ba = ba_ref[0]
    beta_all = _sigmoid(ba)
    g_all = -jnp.exp(garow_ref[...]) * _softplus(ba + gbrow_ref[...])
    ri_t = lax.broadcasted_iota(jnp.int32, (tile, tile), 0)
    ci_t = lax.broadcasted_iota(jnp.int32, (tile, tile), 1)
    cum_mat = ((ri_t // ck == ci_t // ck) & (ri_t >= ci_t)).astype(BF16)
    g_hi = g_all.astype(BF16)
    g_r1 = g_all - g_hi.astype(F32)
    g_mid = g_r1.astype(BF16)
    g_lo = (g_r1 - g_mid.astype(F32)).astype(BF16)
    gc_all = _dot(cum_mat, g_hi) + (_dot(cum_mat, g_mid) + _dot(cum_mat, g_lo))
    gc_all_t = gc_all.T

    ri = lax.broadcasted_iota(jnp.int32, (ck, ck), 0)
    ci = lax.broadcasted_iota(jnp.int32, (ck, ck), 1)
    hnw = hnw_ref[...]
    z_all = z_ref[0]

    prep = []
    for n in range(nck):
        sl = slice(n * ck, (n + 1) * ck)
        for h in range(heads):
            q = qkv[sl, h * dh:(h + 1) * dh]
            k = qkv[sl, width + h * dh:width + (h + 1) * dh]
            v = qkv[sl, 2 * width + h * dh:2 * width + (h + 1) * dh]
            q = q * lax.rsqrt(jnp.sum(q * q, axis=-1, keepdims=True) + NORM_EPS) * (dh ** -0.5)
            k = k * lax.rsqrt(jnp.sum(k * k, axis=-1, keepdims=True) + NORM_EPS)
            beta = beta_all[sl, h:h + 1]
            gcc = gc_all[sl, heads + h:heads + h + 1]
            gcr = gc_all_t[heads + h:heads + h + 1, sl]
            g_last = gcc[ck - 1:ck, :]
            decay = jnp.where(ri >= ci, jnp.exp(jnp.minimum(gcc - gcr, 0.0)), 0.0)
            k_beta = k * beta
            kb16, k16 = k_beta.astype(BF16), k.astype(BF16)
            egc = jnp.exp(gcc)
            prep.append(dict(
                a=jnp.where(ri > ci, _dot_nt(kb16, k16) * decay, 0.0),
                rhs=jnp.concatenate([v * beta, k_beta * egc], axis=-1),
                attn=(_dot_nt(q.astype(BF16), k16) * decay).astype(BF16),
                q_dec=(q * egc).astype(BF16),
                k_dec=(k * jnp.exp(g_last - gcc)).astype(BF16),
                s_decay=jnp.exp(g_last)))
    invs = _unit_lower_inverses([p["a"] for p in prep], ri, ci)
    sols = [_mm3(t, p["rhs"]) for t, p in zip(invs, prep)]

    states = [state_ref[h] for h in range(heads)]
    outs = [[] for _ in range(heads)]
    for n in range(nck):
        cur = [(prep[n * heads + h], sols[n * heads + h]) for h in range(heads)]
        s16 = [st.astype(BF16) for st in states]
        vn16 = [(sol[:, :dh] - _dot(sol[:, dh:].astype(BF16), s)).astype(BF16) for (_, sol), s in zip(cur, s16)]
        for h in range(heads):
            p = cur[h][0]
            outs[h].append(_dot(p["q_dec"], s16[h]) + _dot(p["attn"], vn16[h]))
        states = [st * cur[h][0]["s_decay"] + _dot_tn(cur[h][0]["k_dec"], vn16[h]) for h, st in enumerate(states)]
    for h in range(heads):
        state_ref[h] = states[h]
        o_h = jnp.concatenate(outs[h], axis=0) if nck > 1 else outs[h][0]
        o_h = (_rms(o_h) * hnw) * _silu(z_all[:, h * dh:(h + 1) * dh])
        o_ref[0, :, h * dh:(h + 1) * dh] = o_h.astype(o_ref.dtype)


def _gdn(qkv_pre, z, ba, conv_w, a_log, dt_bias, head_norm_w):
    bsz, seq, w3 = qkv_pre.shape
    width = w3 // 3
    heads = width // GDN_HEAD_DIM
    tile = min(GDN_TILE, seq)
    garow = jnp.zeros((1, LANES), F32).at[0, heads:2 * heads].set(a_log.astype(F32))
    gbrow = jnp.zeros((1, LANES), F32).at[0, heads:2 * heads].set(dt_bias.astype(F32))
    row = lambda b, t: (b, t, 0)
    fix2 = lambda b, t: (0, 0)
    return pl.pallas_call(
        functools.partial(_gdn_kernel, heads=heads, tile=tile),
        grid=(bsz, seq // tile),
        in_specs=[pl.BlockSpec((1, tile, w3), row),
                  pl.BlockSpec((1, tile, width), row),
                  pl.BlockSpec((1, tile, LANES), row),
                  pl.BlockSpec((CONV_K, w3), fix2),
                  pl.BlockSpec((1, LANES), fix2),
                  pl.BlockSpec((1, LANES), fix2),
                  pl.BlockSpec((1, GDN_HEAD_DIM), fix2)],
        out_specs=pl.BlockSpec((1, tile, width), row),
        out_shape=jax.ShapeDtypeStruct((bsz, seq, width), BF16),
        scratch_shapes=[pltpu.VMEM((SUBLANES + tile, w3), F32),
                        pltpu.VMEM((heads, GDN_HEAD_DIM, GDN_HEAD_DIM), F32)],
        compiler_params=_cparams("arbitrary", "arbitrary"),
        name="gated_deltanet",
    )(qkv_pre, z, ba, conv_w.astype(F32), garow, gbrow, head_norm_w.astype(F32).reshape(1, -1))


def _s5_prep_kernel(lr_ref, li_ref, ldt_ref, btr_ref, bti_ref, cr_ref, ci_ref,
                    kst_ref, bjr_ref, bji_ref, cor_ref, coi_ref, l16r_ref, l16i_ref):
    nt = S5_CHUNK
    lr, li = lr_ref[0], li_ref[0]
    dt = jnp.exp(ldt_ref[0])
    mag = jnp.exp(lr * dt)
    lb_re, lb_im = mag * jnp.cos(li * dt), mag * jnp.sin(li * dt)
    den = lr * lr + li * li
    num_re, num_im = lb_re - 1.0, lb_im
    f_re = (num_re * lr + num_im * li) / den
    f_im = (num_im * lr - num_re * li) / den
    btr, bti = btr_ref[0], bti_ref[0]
    bb_re = f_re * btr - f_im * bti
    bb_im = f_re * bti + f_im * btr
    cr, ci = cr_ref[0], ci_ref[0]
    p_re, p_im = [jnp.ones_like(lr)], [jnp.zeros_like(lr)]
    for _ in range(nt):
        p_re.append(p_re[-1] * lb_re - p_im[-1] * lb_im)
        p_im.append(p_re[-2] * lb_im + p_im[-1] * lb_re)
    cp_re = [cr * pr - ci * pi for pr, pi in zip(p_re, p_im)]
    cp_im = [cr * pi + ci * pr for pr, pi in zip(p_re, p_im)]
    kst_ref[0] = (_mm3_nt(jnp.concatenate(cp_re[:nt], axis=0), bb_re)
                  - _mm3_nt(jnp.concatenate(cp_im[:nt], axis=0), bb_im))
    bjr_ref[0] = jnp.concatenate([bb_re * p_re[nt - 1 - s] - bb_im * p_im[nt - 1 - s] for s in range(nt)], axis=0)
    bji_ref[0] = jnp.concatenate([bb_re * p_im[nt - 1 - s] + bb_im * p_re[nt - 1 - s] for s in range(nt)], axis=0)
    cor_ref[0] = jnp.concatenate(cp_re[1:nt + 1], axis=0)
    coi_ref[0] = jnp.concatenate(cp_im[1:nt + 1], axis=0)
    l16r_ref[0] = p_re[nt]
    l16i_ref[0] = p_im[nt]


def _s5_prep(lam_re, lam_im, log_dt, b_re, b_im, c_re, c_im):
    g, p = lam_re.shape
    hs = c_re.shape[1]
    nt = S5_CHUNK
    vec = pl.BlockSpec((1, 1, p), lambda i: (i, 0, 0))
    mat = pl.BlockSpec((1, hs, p), lambda i: (i, 0, 0))
    big = pl.BlockSpec((1, nt * hs, p), lambda i: (i, 0, 0))
    f32 = lambda a: a.astype(F32)
    return pl.pallas_call(
        _s5_prep_kernel,
        grid=(g,),
        in_specs=[vec, vec, pl.BlockSpec((1, 1, 1), lambda i: (i, 0, 0)), mat, mat, mat, mat],
        out_specs=[pl.BlockSpec((1, nt * hs, hs), lambda i: (i, 0, 0)), big, big, big, big, vec, vec],
        out_shape=[jax.ShapeDtypeStruct((g, nt * hs, hs), F32)]
                  + [jax.ShapeDtypeStruct((g, nt * hs, p), F32)] * 4
                  + [jax.ShapeDtypeStruct((g, 1, p), F32)] * 2,
        compiler_params=_cparams("arbitrary"),
        name="s5_discretise",
    )(f32(lam_re).reshape(g, 1, p), f32(lam_im).reshape(g, 1, p), f32(log_dt).reshape(g, 1, 1),
      f32(b_re).transpose(0, 2, 1), f32(b_im).transpose(0, 2, 1), f32(c_re), f32(c_im))


def _s5_kernel(u_ref, toep_ref, binj_ref, ct_ref, dt_ref, l16r_ref, l16i_ref, y_ref,
               sre_ref, sim_ref, xre_ref, xim_ref, *, streams):
    rows = u_ref.shape[1]
    steps = rows // streams
    u = u_ref[0]
    ub = u.astype(BF16)
    inj = _dot(ub, binj_ref[0])
    sre_ref[...] = inj[:, :LANES]
    sim_ref[...] = inj[:, LANES:]
    lr = jnp.broadcast_to(l16r_ref[0], (streams, LANES))
    li = jnp.broadcast_to(l16i_ref[0], (streams, LANES))

    def scan_step(i, carry):
        xr, xi = carry
        at = pl.ds(i, streams, stride=steps)
        xre_ref[at, :] = xr
        xim_ref[at, :] = xi
        sr = sre_ref[at, :]
        si = sim_ref[at, :]
        return lr * xr - li * xi + sr, lr * xi + li * xr + si

    zero = jnp.zeros((streams, LANES), F32)
    lax.fori_loop(0, steps, scan_step, (zero, zero), unroll=8)

    xin = jnp.concatenate([xre_ref[...], xim_ref[...]], axis=1).astype(BF16)
    y_ref[0] = _dot(ub, toep_ref[0]) + _dot(xin, ct_ref[0]) + dt_ref[0] * u


def _s5(ut, bsz, lam_re, lam_im, log_dt, b_re, b_im, c_re, c_im, d_skip):
    g, rows, _ = ut.shape
    hs, p, nt = S5_GROUP, S5_STATE, S5_CHUNK
    streams = bsz
    assert p * 2 == LANES and nt * hs == 2 * LANES
    steps = rows // streams
    kst, bjr, bji, cor, coi, l16r, l16i = _s5_prep(lam_re, lam_im, log_dt, b_re, b_im, c_re, c_im)

    k4 = kst.reshape(g, nt, hs, hs)
    toep = jnp.stack([jnp.pad(k4[:, :nt - s], ((0, 0), (s, 0), (0, 0), (0, 0))) for s in range(nt)],
                     axis=1)
    toep = toep.transpose(0, 1, 4, 2, 3).reshape(g, nt * hs, nt * hs).astype(BF16)
    zpad = jnp.zeros((g, nt * hs, LANES - p), F32)
    binj = jnp.concatenate([bjr, zpad, bji, zpad], axis=-1).astype(BF16)
    zrow = jnp.zeros((g, LANES - p, nt * hs), F32)
    ct = jnp.concatenate([cor.transpose(0, 2, 1), zrow, -coi.transpose(0, 2, 1), zrow], axis=1).astype(BF16)
    dtile = jnp.tile(d_skip.astype(F32).reshape(g, 1, hs), (1, 1, nt))
    lpad = jnp.zeros((g, 1, LANES - p), F32)
    l16r = jnp.concatenate([l16r, lpad], axis=-1)
    l16i = jnp.concatenate([l16i, lpad], axis=-1)
    toep = _to_lane_order(_to_lane_order(toep, 1), 2)
    binj = _to_lane_order(binj, 1)
    ct = _to_lane_order(ct, 2)
    dtile = _to_lane_order(dtile, 2)

    per_g3 = lambda i: (i, 0, 0)
    return pl.pallas_call(
        functools.partial(_s5_kernel, streams=streams),
        grid=(g,),
        in_specs=[pl.BlockSpec((1, rows, nt * hs), per_g3),
                  pl.BlockSpec((1, nt * hs, nt * hs), per_g3),
                  pl.BlockSpec((1, nt * hs, 2 * LANES), per_g3),
                  pl.BlockSpec((1, 2 * LANES, nt * hs), per_g3),
                  pl.BlockSpec((1, 1, nt * hs), per_g3),
                  pl.BlockSpec((1, 1, LANES), per_g3),
                  pl.BlockSpec((1, 1, LANES), per_g3)],
        out_specs=pl.BlockSpec((1, rows, nt * hs), per_g3),
        out_shape=jax.ShapeDtypeStruct((g, rows, nt * hs), F32),
        scratch_shapes=[pltpu.VMEM((rows, LANES), F32)] * 4,
        compiler_params=_cparams("arbitrary"),
        name="s5_chunked_scan",
    )(ut, toep, binj, ct, dtile, l16r, l16i)


def _sb_kernel(q_ref, k_ref, v_ref, o_ref, acc_ref, run_ref, *, qb, kb, nsub):
    qi = pl.program_id(2)
    per_key = kb // qb
    pairs = nsub // per_key
    ri = lax.broadcasted_iota(jnp.int32, (kb, kb), 0)
    ci = lax.broadcasted_iota(jnp.int32, (kb, kb), 1)
    after = (ri > ci).astype(BF16)
    rq = lax.broadcasted_iota(jnp.int32, (qb, kb), 0)
    cq = lax.broadcasted_iota(jnp.int32, (qb, kb), 1)
    diag_valid = [cq < rq + r * qb for r in range(per_key)]
    base = pairs * qi

    def pre(sub, kblk, valid):
        s = _dot_nt(q_ref[0, sub * qb:(sub + 1) * qb, :], kblk)
        lse = jnp.log2(1.0 + jnp.exp2(-jnp.abs(s)))
        log_p = jnp.minimum(s, 0.0) - lse
        neg_log_1m = jnp.maximum(s, 0.0) + lse
        if valid is not None:
            neg_log_1m = jnp.where(valid, neg_log_1m, 0.0)
        hi, lo = _split2(neg_log_1m)
        later = _dot(hi, after) + _dot(lo, after)
        return log_p - later, jnp.sum(neg_log_1m, axis=1, keepdims=True)

    def post(sub, part, tot, vblk, valid, exists):
        rows = slice(sub * qb, (sub + 1) * qb)
        wts = jnp.exp2(part - run_ref[rows, :])
        if valid is not None:
            wts = jnp.where(valid, wts, 0.0)
        if exists is not None:
            wts = jnp.where(exists, wts, 0.0)
            tot = jnp.where(exists, tot, 0.0)
        acc_ref[rows, :] += _dot(wts.astype(BF16), vblk)
        run_ref[rows, :] += tot

    def band_pre(e, diagonal):
        items = []
        for m in range(pairs):
            idx = base + m - e
            exists = None if diagonal else idx >= 0
            k0 = pl.multiple_of(jnp.maximum(idx, 0) * kb, kb)
            kblk, vblk = k_ref[0, pl.ds(k0, kb), :], v_ref[0, pl.ds(k0, kb), :]
            for r in range(per_key):
                sub = m * per_key + r
                valid = diag_valid[r] if diagonal else None
                items.append((sub, vblk, valid, exists, pre(sub, kblk, valid)))
        return items

    def band_post(items):
        for sub, vblk, valid, exists, (part, tot) in items:
            post(sub, part, tot, vblk, valid, exists)

    acc_ref[...] = jnp.zeros(acc_ref.shape, F32)
    run_ref[...] = jnp.zeros(run_ref.shape, F32)
    first = [band_pre(e, e == 0) for e in range(SB_STATIC_BANDS)]
    for items in first:
        band_post(items)

    last = base + pairs - 1

    def keep_going(state):
        e, run_min = state
        return jnp.logical_and(e <= last, run_min < SB_UNDERFLOW_LOG2)

    def body(state):
        e, _ = state
        band_post(band_pre(e, False))
        return e + 1, jnp.min(run_ref[...])

    lax.while_loop(keep_going, body, (jnp.int32(SB_STATIC_BANDS), jnp.min(run_ref[...])))
    o_ref[0] = acc_ref[...].astype(o_ref.dtype)


def _sb_attention(q, kv, heads):
    bsz, seq, _ = q.shape
    d = SB_HEAD_DIM
    kb = min(SB_KEY_BLOCK, seq)
    qb = min(SB_QUERY_BLOCK, kb)
    tq = min(SB_Q_TILE, seq)
    nsub = tq // qb
    return pl.pallas_call(
        functools.partial(_sb_kernel, qb=qb, kb=kb, nsub=nsub),
        grid=(bsz, heads, seq // tq),
        in_specs=[pl.BlockSpec((1, tq, d), lambda b, h, i: (b, i, h)),
                  pl.BlockSpec((1, seq, d), lambda b, h, i: (b, 0, h)),
                  pl.BlockSpec((1, seq, d), lambda b, h, i: (b, 0, heads + h))],
        out_specs=pl.BlockSpec((1, tq, d), lambda b, h, i: (b, i, h)),
        out_shape=jax.ShapeDtypeStruct((bsz, seq, heads * d), BF16),
        scratch_shapes=[pltpu.VMEM((tq, d), F32), pltpu.VMEM((tq, 1), F32)],
        compiler_params=_cparams("arbitrary", "arbitrary", "arbitrary"),
        name="stick_breaking_attention",
    )(q, kv, kv)


def _gelu_tanh(x):
    return 0.5 * x * (1.0 + jnp.tanh(0.7978845608028654 * (x + 0.044715 * (x * x * x))))


def _mix_ffn_kernel(*refs, even, final, f_chunks):
    if even:
        (h_ref, o_ref, y_ref, gw_ref, gb_ref, wtop_ref, wbot_ref, gtm_ref, nw_ref, sc_ref, sh_ref, gtf_ref,
         win_ref, wout_ref, fw_ref, out_ref, ybuf_ref) = refs
        y = _gelu_tanh(_from_chunk_major(y_ref, ybuf_ref))
        y = y * _sigmoid(_dot(y.astype(BF16), gw_ref[...]) + gb_ref[...])
        mix = _dot(o_ref[0], wtop_ref[...]) + _dot(y.astype(BF16), wbot_ref[...])
    else:
        (h_ref, o_ref, wtop_ref, gtm_ref, nw_ref, sc_ref, sh_ref, gtf_ref,
         win_ref, wout_ref, fw_ref, out_ref) = refs
        mix = _dot(o_ref[0], wtop_ref[...])
    h1 = h_ref[0] + gtm_ref[0] * mix
    hn = _norm_mod(h1, nw_ref[...], sc_ref[0], sh_ref[0]).astype(BF16)
    hidden = wout_ref.shape[0]
    fc = hidden // f_chunks
    acc = None
    for c in range(f_chunks):
        gate = _dot(hn, win_ref[:, c * fc:(c + 1) * fc])
        up = _dot(hn, win_ref[:, hidden + c * fc:hidden + (c + 1) * fc])
        part = _dot((_silu(gate) * up).astype(BF16), wout_ref[c * fc:(c + 1) * fc, :])
        acc = part if acc is None else acc + part
    h2 = h1 + gtf_ref[0] * acc
    if final:
        h2 = _rms(h2) * fw_ref[...]
    out_ref[0] = h2


def _mix_ffn(h, acts, mix_w, glu, gt_m, nw, sc, sh, gt_f, w_in, w_out, final_w, *, even, final):
    bsz, seq, d = h.shape
    tm = min(ROW_TILE, seq)
    hidden = w_out.shape[0]
    row = lambda b, t: (b, t, 0)
    per_b = lambda b, t: (b, 0, 0)
    vec_d = pl.BlockSpec((1, d), lambda b, t: (0, 0))
    mod_spec = pl.BlockSpec((1, 1, d), per_b)
    args, specs, scratch = [h], [pl.BlockSpec((1, tm, d), row)], []
    if even:
        o, y = acts
        half = o.shape[-1]
        nt = seq // tm
        glu_w, glu_b = glu
        args += [o, y, glu_w.astype(BF16), glu_b.astype(F32).reshape(1, -1),
                 mix_w[:half].astype(BF16), mix_w[half:].astype(BF16)]
        specs += [pl.BlockSpec((1, tm, half), row),
                  pl.BlockSpec((y.shape[0], tm // S5_CHUNK, y.shape[2]), lambda b, t: (0, b * nt + t, 0)),
                  _const_spec(glu_w.shape), pl.BlockSpec((1, glu_b.shape[-1]), lambda b, t: (0, 0)),
                  _const_spec((half, d)), _const_spec((d - half, d))]
        scratch = [pltpu.VMEM(((d - half) // LANES, tm, LANES), F32)]
    else:
        (o,) = acts
        args += [o, mix_w.astype(BF16)]
        specs += [pl.BlockSpec((1, tm, d), row), _const_spec((d, d))]
    args += [gt_m, nw.astype(F32).reshape(1, d), sc, sh, gt_f, w_in.astype(BF16), w_out.astype(BF16),
             final_w.astype(F32).reshape(1, d)]
    specs += [mod_spec, vec_d, mod_spec, mod_spec, mod_spec, _const_spec(w_in.shape), _const_spec(w_out.shape), vec_d]
    return pl.pallas_call(
        functools.partial(_mix_ffn_kernel, even=even, final=final, f_chunks=2),
        grid=(bsz, seq // tm),
        in_specs=specs,
        out_specs=pl.BlockSpec((1, tm, d), row),
        out_shape=jax.ShapeDtypeStruct((bsz, seq, d), F32),
        scratch_shapes=scratch,
        compiler_params=_cparams("arbitrary", "arbitrary"),
        name="outproj_swiglu_ffn",
    )(*args)


def kernel(x, c, ada_w, ada_b, norm_mix_w, norm_ffn_w, ffn_w_in, ffn_w_out, hy_w_in, hy_conv_w, hy_a_log, hy_dt_bias, hy_head_norm_w, s5_lam_re, s5_lam_im, s5_log_dt, s5_b_re, s5_b_im, s5_c_re, s5_c_im, s5_d, s5_glu_w, s5_glu_b, hy_w_out, sb_w_in, sb_w_out, final_norm_w):
    bsz, seq, d = x.shape
    depth = ada_w.shape[0]
    gdn_width = d // 2
    gdn_heads = gdn_width // GDN_HEAD_DIM
    s5_width = d - gdn_width
    sb_heads = d // SB_HEAD_DIM

    mod = _modulation(c, ada_w, ada_b)
    h = x
    for i in range(depth):
        sh_m, sc_m, gt_m, sh_f, sc_f, gt_f = (mod[i, :bsz, k * d:(k + 1) * d].reshape(bsz, 1, d) for k in range(6))
        j = i // 2
        last = i == depth - 1
        if i % 2 == 0:
            w = hy_w_in[j]
            w4 = 4 * gdn_width
            gate_cols = jnp.pad(w[:, w4:w4 + 2 * gdn_heads], ((0, 0), (0, LANES - 2 * gdn_heads)))
            w_cat = jnp.concatenate([w[:, :w4], gate_cols, w[:, w4 + 2 * gdn_heads:]], axis=1).astype(BF16)
            qkv_pre, zg, ba, ut = _inproj(h, norm_mix_w[i], sc_m, sh_m, w_cat,
                                          (3 * gdn_width, gdn_width, LANES, s5_width), (F32, F32, F32, F32),
                                          chunk_major_last=True)
            o = _gdn(qkv_pre, zg, ba, hy_conv_w[j], hy_a_log[j], hy_dt_bias[j], hy_head_norm_w[j])
            y = _s5(ut, bsz, s5_lam_re[j], s5_lam_im[j], s5_log_dt[j], s5_b_re[j], s5_b_im[j],
                    s5_c_re[j], s5_c_im[j], s5_d[j])
            h = _mix_ffn(h, (o, y), hy_w_out[j], (s5_glu_w[j], s5_glu_b[j]), gt_m, norm_ffn_w[i], sc_f, sh_f, gt_f,
                         ffn_w_in[i], ffn_w_out[i], final_norm_w, even=True, final=last)
        else:
            q, kv = _inproj(h, norm_mix_w[i], sc_m, sh_m, sb_w_in[j].astype(BF16), (d, 2 * d), (BF16, BF16),
                            scales=(SB_HEAD_DIM ** -0.5 * LOG2_E, 1.0))
            o = _sb_attention(q, kv, sb_heads)
            h = _mix_ffn(h, (o,), sb_w_out[j], None, gt_m, norm_ffn_w[i], sc_f, sh_f, gt_f,
                         ffn_w_in[i], ffn_w_out[i], final_norm_w, even=False, final=last)
    return h
```

```python
import functools

import jax
import jax.numpy as jnp
import numpy as np
from jax import lax
from jax.experimental import pallas as pl
from jax.experimental.pallas import tpu as pltpu

F32 = jnp.float32
BF16 = jnp.bfloat16
NORM_EPS = 1e-6

GDN_HEAD_DIM = 128
CONV_K = 4
GDN_CHUNK = 64
S5_GROUP = 16
S5_STATE = 64
SB_HEAD_DIM = 128

LANES = 128
SUBLANES = 8
VMEM_LIMIT_BYTES = 56 * 1024 * 1024

ROW_TILE = 512
GDN_TILE = 256
S5_CHUNK = 16
SB_KEY_BLOCK = 256
SB_QUERY_BLOCK = 128
SB_Q_TILE = 2048
SB_STATIC_BANDS = 2
SB_NO_KEYS = 1.0e6
LOG2_E = 1.4426950408889634
SB_UNDERFLOW_LOG2 = 160.0
MOD_COL_TILE = 1536


def _cparams(*sem):
    return pltpu.CompilerParams(dimension_semantics=sem, vmem_limit_bytes=VMEM_LIMIT_BYTES)


def _const_spec(shape):
    nd = len(shape)
    return pl.BlockSpec(shape, lambda *_: (0,) * nd, pipeline_mode=pl.Buffered(1))


def _dot(a, b):
    return jnp.dot(a, b, preferred_element_type=F32)


def _dot_nt(a, b):
    return lax.dot_general(a, b, (((1,), (1,)), ((), ())), preferred_element_type=F32)


def _dot_tn(a, b):
    return lax.dot_general(a, b, (((0,), (0,)), ((), ())), preferred_element_type=F32)


def _split2(x):
    hi = x.astype(BF16)
    lo = (x - hi.astype(F32)).astype(BF16)
    return hi, lo


def _mm3(a, b):
    ah, al = _split2(a)
    bh, bl = _split2(b)
    return _dot(ah, bh) + (_dot(ah, bl) + _dot(al, bh))


def _mm3_nt(a, b):
    ah, al = _split2(a)
    bh, bl = _split2(b)
    return _dot_nt(ah, bh) + (_dot_nt(ah, bl) + _dot_nt(al, bh))


def _sigmoid(x):
    return 1.0 / (1.0 + jnp.exp(-x))


def _silu(x):
    return x * _sigmoid(x)


def _softplus(x):
    return jnp.maximum(x, 0.0) + jnp.log(1.0 + jnp.exp(-jnp.abs(x)))


def _rms(x):
    return x * lax.rsqrt(jnp.mean(x * x, axis=-1, keepdims=True) + NORM_EPS)


def _norm_mod(x, nw, sc, sh):
    return (_rms(x) * nw) * (1.0 + sc) + sh


def _mod_kernel(c_ref, w_ref, b_ref, o_ref):
    ca = _silu(c_ref[...])
    o_ref[0] = _mm3(ca, w_ref[0]) + b_ref[0]


def _modulation(c, ada_w, ada_b):
    depth, d, n = ada_w.shape
    bsz = c.shape[0]
    rows = -(-bsz // SUBLANES) * SUBLANES
    c_pad = jnp.pad(c, ((0, rows - bsz), (0, 0)))
    tn = MOD_COL_TILE
    return pl.pallas_call(
        _mod_kernel,
        grid=(depth, n // tn),
        in_specs=[pl.BlockSpec((rows, d), lambda i, j: (0, 0)),
                  pl.BlockSpec((1, d, tn), lambda i, j: (i, 0, j)),
                  pl.BlockSpec((1, 1, tn), lambda i, j: (i, 0, j))],
        out_specs=pl.BlockSpec((1, rows, tn), lambda i, j: (i, 0, j)),
        out_shape=jax.ShapeDtypeStruct((depth, rows, n), F32),
        compiler_params=_cparams("arbitrary", "arbitrary"),
        name="adaln_modulation",
    )(c_pad, ada_w, ada_b.reshape(depth, 1, n))


def _s5_lane(group, s, hi=0):
    per_block = LANES // S5_GROUP
    return (s // per_block) * LANES + ((s + group) % per_block) * S5_GROUP + hi


def _to_lane_order(a, axis):
    per_block = LANES // S5_GROUP
    shp = a.shape
    a = a.reshape((shp[0] // per_block, per_block) + shp[1:axis]
                  + (S5_CHUNK // per_block, per_block, S5_GROUP) + shp[axis + 1:])
    parts = [jnp.roll(a[:, j], j, axis=axis + 1) for j in range(per_block)]
    return jnp.stack(parts, axis=1).reshape(shp)


def _to_chunk_major(x, z_ref, buf_ref):
    tokens, width = x.shape
    chunks = tokens // S5_CHUNK
    per_block = LANES // S5_GROUP
    for lb in range(width // LANES):
        buf_ref[lb] = x[:, lb * LANES:(lb + 1) * LANES]
    for s in range(S5_CHUNK):
        shift = (s % per_block) * S5_GROUP
        for lb in range(width // LANES):
            rows = buf_ref[lb, pl.ds(s, chunks, stride=S5_CHUNK), :]
            rows = pltpu.roll(rows, shift, axis=1) if shift else rows
            for j in range(per_block):
                src = ((s + j) % per_block) * S5_GROUP
                dst = _s5_lane(lb * per_block + j, s)
                z_ref[lb * per_block + j, :, dst:dst + S5_GROUP] = rows[:, src:src + S5_GROUP]


def _from_chunk_major(z_ref, buf_ref):
    groups, chunks, _ = z_ref.shape
    per_block = LANES // S5_GROUP
    nlb = groups // per_block
    lane_block = lax.broadcasted_iota(jnp.int32, (chunks, LANES), 1) // S5_GROUP
    for s in range(S5_CHUNK):
        half = s // per_block
        shift = (s % per_block) * S5_GROUP
        for lb in range(nlb):
            rows = None
            for j in range(per_block):
                piece = z_ref[lb * per_block + j, :, half * LANES:(half + 1) * LANES]
                rows = piece if rows is None else jnp.where(lane_block == (s + j) % per_block, piece, rows)
            rows = pltpu.roll(rows, LANES - shift, axis=1) if shift else rows
            buf_ref[lb, pl.ds(s, chunks, stride=S5_CHUNK), :] = rows
    return jnp.concatenate([buf_ref[lb] for lb in range(nlb)], axis=1)


def _inproj_kernel(h_ref, nw_ref, sc_ref, sh_ref, w_ref, *refs, widths, scales, chunk_major_last):
    out_refs = refs[:len(widths)]
    hn = _norm_mod(h_ref[0], nw_ref[...], sc_ref[0], sh_ref[0]).astype(BF16)
    off = 0
    for idx, (o_ref, wd, s) in enumerate(zip(out_refs, widths, scales)):
        acc = _dot(hn, w_ref[:, off:off + wd])
        if s != 1.0:
            acc = acc * s
        if chunk_major_last and idx == len(widths) - 1:
            _to_chunk_major(acc, o_ref, refs[len(widths)])
        else:
            o_ref[0] = acc.astype(o_ref.dtype)
        off += wd


def _inproj(h, nw, sc, sh, w, widths, dtypes, scales=None, chunk_major_last=False):
    scales = scales or (1.0,) * len(widths)
    bsz, seq, d = h.shape
    tm = min(ROW_TILE, seq)
    nt = seq // tm
    n = w.shape[1]
    row = lambda b, t: (b, t, 0)
    per_b = lambda b, t: (b, 0, 0)
    out_specs = [pl.BlockSpec((1, tm, wd), row) for wd in widths]
    out_shape = [jax.ShapeDtypeStruct((bsz, seq, wd), dt) for wd, dt in zip(widths, dtypes)]
    scratch = []
    if chunk_major_last:
        groups = widths[-1] // S5_GROUP
        row_w = S5_CHUNK * S5_GROUP
        out_specs[-1] = pl.BlockSpec((groups, tm // S5_CHUNK, row_w), lambda b, t: (0, b * nt + t, 0))
        out_shape[-1] = jax.ShapeDtypeStruct((groups, bsz * seq // S5_CHUNK, row_w), dtypes[-1])
        scratch = [pltpu.VMEM((widths[-1] // LANES, tm, LANES), F32)]
    return pl.pallas_call(
        functools.partial(_inproj_kernel, widths=widths, scales=scales, chunk_major_last=chunk_major_last),
        grid=(bsz, nt),
        in_specs=[pl.BlockSpec((1, tm, d), row),
                  pl.BlockSpec((1, d), lambda b, t: (0, 0)),
                  pl.BlockSpec((1, 1, d), per_b),
                  pl.BlockSpec((1, 1, d), per_b),
                  _const_spec((d, n))],
        out_specs=out_specs,
        out_shape=out_shape,
        scratch_shapes=scratch,
        compiler_params=_cparams("arbitrary", "arbitrary"),
        name="norm_mod_inproj",
    )(h, nw.reshape(1, d), sc, sh, w)


def _unit_lower_inverses(mats, ri, ci):
    n = mats[0].shape[0]
    eye = (ri == ci).astype(F32)
    pair = (ri // 2 == ci // 2) & (ri > ci)
    ts = [eye - jnp.where(pair, a, 0.0) for a in mats]
    s = 2
    while s < n:
        sel = (ri // (2 * s) == ci // (2 * s)) & (ri // s > ci // s)
        tmps = [_mm3(t, jnp.where(sel, a, 0.0)) for t, a in zip(ts, mats)]
        ts = [t - _mm3(tmp, t) for t, tmp in zip(ts, tmps)]
        s *= 2
    return ts


def _gdn_kernel(x_ref, z_ref, ba_ref, cw_ref, garow_ref, gbrow_ref, hnw_ref, o_ref,
                xbuf_ref, state_ref, *, heads, tile):
    dh = GDN_HEAD_DIM
    width = heads * dh
    ck = GDN_CHUNK
    nck = tile // ck
    t_id = pl.program_id(1)

    @pl.when(t_id == 0)
    def _():
        xbuf_ref[0:SUBLANES, :] = jnp.zeros((SUBLANES, 3 * width), F32)
        state_ref[...] = jnp.zeros(state_ref.shape, F32)

    x = x_ref[0]
    xbuf_ref[SUBLANES:SUBLANES + tile, :] = x
    conv = cw_ref[0:1, :] * xbuf_ref[pl.ds(SUBLANES - CONV_K + 1, tile), :]
    for k in range(1, CONV_K):
        conv = conv + cw_ref[k:k + 1, :] * xbuf_ref[pl.ds(SUBLANES - CONV_K + 1 + k, tile), :]
    xbuf_ref[0:SUBLANES, :] = x[tile - SUBLANES:tile, :]
    qkv = _silu(conv)

    ba = ba_ref[0]
    beta_all = _sigmoid(ba)
    g_all = -jnp.exp(garow_ref[...]) * _softplus(ba + gbrow_ref[...])
    ri_t = lax.broadcasted_iota(jnp.int32, (tile, tile), 0)
    ci_t = lax.broadcasted_iota(jnp.int32, (tile, tile), 1)
    cum_mat = ((ri_t // ck == ci_t // ck) & (ri_t >= ci_t)).astype(BF16)
    g_hi = g_all.astype(BF16)
    g_r1 = g_all - g_hi.astype(F32)
    g_mid = g_r1.astype(BF16)
    g_lo = (g_r1 - g_mid.astype(F32)).astype(BF16)
    gc_all = _dot(cum_mat, g_hi) + (_dot(cum_mat, g_mid) + _dot(cum_mat, g_lo))
    gc_all_t = gc_all.T

    ri = lax.broadcasted_iota(jnp.int32, (ck, ck), 0)
    ci = lax.broadcasted_iota(jnp.int32, (ck, ck), 1)
    hnw = hnw_ref[...]
    z_all = z_ref[0]

    prep = []
    for n in range(nck):
        sl = slice(n * ck, (n + 1) * ck)
        for h in range(heads):
            q = qkv[sl, h * dh:(h + 1) * dh]
            k = qkv[sl, width + h * dh:width + (h + 1) * dh]
            v = qkv[sl, 2 * width + h * dh:2 * width + (h + 1) * dh]
            q = q * lax.rsqrt(jnp.sum(q * q, axis=-1, keepdims=True) + NORM_EPS) * (dh ** -0.5)
            k = k * lax.rsqrt(jnp.sum(k * k, axis=-1, keepdims=True) + NORM_EPS)
            beta = beta_all[sl, h:h + 1]
            gcc = gc_all[sl, heads + h:heads + h + 1]
            gcr = gc_all_t[heads + h:heads + h + 1, sl]
            g_last = gcc[ck - 1:ck, :]
            decay = jnp.where(ri >= ci, jnp.exp(jnp.minimum(gcc - gcr, 0.0)), 0.0)
            k_beta = k * beta
            kb16, k16 = k_beta.astype(BF16), k.astype(BF16)
            egc = jnp.exp(gcc)
            prep.append(dict(
                a=jnp.where(ri > ci, _dot_nt(kb16, k16) * decay, 0.0),
                rhs=jnp.concatenate([v * beta, k_beta * egc], axis=-1),
                attn=(_dot_nt(q.astype(BF16), k16) * decay).astype(BF16),
                q_dec=(q * egc).astype(BF16),
                k_dec=(k * jnp.exp(g_last - gcc)).astype(BF16),
                s_decay=jnp.exp(g_last)))
    invs = _unit_lower_inverses([p["a"] for p in prep], ri, ci)
    sols = [_mm3(t, p["rhs"]) for t, p in zip(invs, prep)]

    states = [state_ref[h] for h in range(heads)]
    outs = [[] for _ in range(heads)]
    for n in range(nck):
        cur = [(prep[n * heads + h], sols[n * heads + h]) for h in range(heads)]
        s16 = [st.astype(BF16) for st in states]
        vn16 = [(sol[:, :dh] - _dot(sol[:, dh:].astype(BF16), s)).astype(BF16) for (_, sol), s in zip(cur, s16)]
        for h in range(heads):
            p = cur[h][0]
            outs[h].append(_dot(p["q_dec"], s16[h]) + _dot(p["attn"], vn16[h]))
        states = [st * cur[h][0]["s_decay"] + _dot_tn(cur[h][0]["k_dec"], vn16[h]) for h, st in enumerate(states)]
    for h in range(heads):
        state_ref[h] = states[h]
        o_h = jnp.concatenate(outs[h], axis=0) if nck > 1 else outs[h][0]
        o_h = (_rms(o_h) * hnw) * _silu(z_all[:, h * dh:(h + 1) * dh])
        o_ref[0, :, h * dh:(h + 1) * dh] = o_h.astype(o_ref.dtype)


def _gdn(qkv_pre, z, ba, conv_w, a_log, dt_bias, head_norm_w):
    bsz, seq, w3 = qkv_pre.shape
    width = w3 // 3
    heads = width // GDN_HEAD_DIM
    tile = min(GDN_TILE, seq)
    garow = jnp.zeros((1, LANES), F32).at[0, heads:2 * heads].set(a_log.astype(F32))
    gbrow = jnp.zeros((1, LANES), F32).at[0, heads:2 * heads].set(dt_bias.astype(F32))
    row = lambda b, t: (b, t, 0)
    fix2 = lambda b, t: (0, 0)
    return pl.pallas_call(
        functools.partial(_gdn_kernel, heads=heads, tile=tile),
        grid=(bsz, seq // tile),
        in_specs=[pl.BlockSpec((1, tile, w3), row),
                  pl.BlockSpec((1, tile, width), row),
                  pl.BlockSpec((1, tile, LANES), row),
                  pl.BlockSpec((CONV_K, w3), fix2),
                  pl.BlockSpec((1, LANES), fix2),
                  pl.BlockSpec((1, LANES), fix2),
                  pl.BlockSpec((1, GDN_HEAD_DIM), fix2)],
        out_specs=pl.BlockSpec((1, tile, width), row),
        out_shape=jax.ShapeDtypeStruct((bsz, seq, width), BF16),
        scratch_shapes=[pltpu.VMEM((SUBLANES + tile, w3), F32),
                        pltpu.VMEM((heads, GDN_HEAD_DIM, GDN_HEAD_DIM), F32)],
        compiler_params=_cparams("arbitrary", "arbitrary"),
        name="gated_deltanet",
    )(qkv_pre, z, ba, conv_w.astype(F32), garow, gbrow, head_norm_w.astype(F32).reshape(1, -1))


def _s5_prep_kernel(lr_ref, li_ref, ldt_ref, btr_ref, bti_ref, cr_ref, ci_ref,
                    kst_ref, bjr_ref, bji_ref, cor_ref, coi_ref, l16r_ref, l16i_ref):
    nt = S5_CHUNK
    lr, li = lr_ref[0], li_ref[0]
    dt = jnp.exp(ldt_ref[0])
    mag = jnp.exp(lr * dt)
    lb_re, lb_im = mag * jnp.cos(li * dt), mag * jnp.sin(li * dt)
    den = lr * lr + li * li
    num_re, num_im = lb_re - 1.0, lb_im
    f_re = (num_re * lr + num_im * li) / den
    f_im = (num_im * lr - num_re * li) / den
    btr, bti = btr_ref[0], bti_ref[0]
    bb_re = f_re * btr - f_im * bti
    bb_im = f_re * bti + f_im * btr
    cr, ci = cr_ref[0], ci_ref[0]
    p_re, p_im = [jnp.ones_like(lr)], [jnp.zeros_like(lr)]
    for _ in range(nt):
        p_re.append(p_re[-1] * lb_re - p_im[-1] * lb_im)
        p_im.append(p_re[-2] * lb_im + p_im[-1] * lb_re)
    cp_re = [cr * pr - ci * pi for pr, pi in zip(p_re, p_im)]
    cp_im = [cr * pi + ci * pr for pr, pi in zip(p_re, p_im)]
    kst_ref[0] = (_mm3_nt(jnp.concatenate(cp_re[:nt], axis=0), bb_re)
                  - _mm3_nt(jnp.concatenate(cp_im[:nt], axis=0), bb_im))
    bjr_ref[0] = jnp.concatenate([bb_re * p_re[nt - 1 - s] - bb_im * p_im[nt - 1 - s] for s in range(nt)], axis=0)
    bji_ref[0] = jnp.concatenate([bb_re * p_im[nt - 1 - s] + bb_im * p_re[nt - 1 - s] for s in range(nt)], axis=0)
    cor_ref[0] = jnp.concatenate(cp_re[1:nt + 1], axis=0)
    coi_ref[0] = jnp.concatenate(cp_im[1:nt + 1], axis=0)
    l16r_ref[0] = p_re[nt]
    l16i_ref[0] = p_im[nt]


def _s5_prep(lam_re, lam_im, log_dt, b_re, b_im, c_re, c_im):
    g, p = lam_re.shape
    hs = c_re.shape[1]
    nt = S5_CHUNK
    vec = pl.BlockSpec((1, 1, p), lambda i: (i, 0, 0))
    mat = pl.BlockSpec((1, hs, p), lambda i: (i, 0, 0))
    big = pl.BlockSpec((1, nt * hs, p), lambda i: (i, 0, 0))
    f32 = lambda a: a.astype(F32)
    return pl.pallas_call(
        _s5_prep_kernel,
        grid=(g,),
        in_specs=[vec, vec, pl.BlockSpec((1, 1, 1), lambda i: (i, 0, 0)), mat, mat, mat, mat],
        out_specs=[pl.BlockSpec((1, nt * hs, hs), lambda i: (i, 0, 0)), big, big, big, big, vec, vec],
        out_shape=[jax.ShapeDtypeStruct((g, nt * hs, hs), F32)]
                  + [jax.ShapeDtypeStruct((g, nt * hs, p), F32)] * 4
                  + [jax.ShapeDtypeStruct((g, 1, p), F32)] * 2,
        compiler_params=_cparams("arbitrary"),
        name="s5_discretise",
    )(f32(lam_re).reshape(g, 1, p), f32(lam_im).reshape(g, 1, p), f32(log_dt).reshape(g, 1, 1),
      f32(b_re).transpose(0, 2, 1), f32(b_im).transpose(0, 2, 1), f32(c_re), f32(c_im))


def _s5_kernel(u_ref, toep_ref, binj_ref, ct_ref, dt_ref, l16r_ref, l16i_ref, y_ref,
               sre_ref, sim_ref, xre_ref, xim_ref, *, streams):
    rows = u_ref.shape[1]
    steps = rows // streams
    u = u_ref[0]
    ub = u.astype(BF16)
    inj = _dot(ub, binj_ref[0])
    sre_ref[...] = inj[:, :LANES]
    sim_ref[...] = inj[:, LANES:]
    lr = jnp.broadcast_to(l16r_ref[0], (streams, LANES))
    li = jnp.broadcast_to(l16i_ref[0], (streams, LANES))

    def scan_step(i, carry):
        xr, xi = carry
        at = pl.ds(i, streams, stride=steps)
        xre_ref[at, :] = xr
        xim_ref[at, :] = xi
        sr = sre_ref[at, :]
        si = sim_ref[at, :]
        return lr * xr - li * xi + sr, lr * xi + li * xr + si

    zero = jnp.zeros((streams, LANES), F32)
    lax.fori_loop(0, steps, scan_step, (zero, zero), unroll=8)

    xin = jnp.concatenate([xre_ref[...], xim_ref[...]], axis=1).astype(BF16)
    y_ref[0] = _dot(ub, toep_ref[0]) + _dot(xin, ct_ref[0]) + dt_ref[0] * u


def _s5(ut, bsz, lam_re, lam_im, log_dt, b_re, b_im, c_re, c_im, d_skip):
    g, rows, _ = ut.shape
    hs, p, nt = S5_GROUP, S5_STATE, S5_CHUNK
    streams = bsz
    assert p * 2 == LANES and nt * hs == 2 * LANES
    steps = rows // streams
    kst, bjr, bji, cor, coi, l16r, l16i = _s5_prep(lam_re, lam_im, log_dt, b_re, b_im, c_re, c_im)

    k4 = kst.reshape(g, nt, hs, hs)
    toep = jnp.stack([jnp.pad(k4[:, :nt - s], ((0, 0), (s, 0), (0, 0), (0, 0))) for s in range(nt)],
                     axis=1)
    toep = toep.transpose(0, 1, 4, 2, 3).reshape(g, nt * hs, nt * hs).astype(BF16)
    zpad = jnp.zeros((g, nt * hs, LANES - p), F32)
    binj = jnp.concatenate([bjr, zpad, bji, zpad], axis=-1).astype(BF16)
    zrow = jnp.zeros((g, LANES - p, nt * hs), F32)
    ct = jnp.concatenate([cor.transpose(0, 2, 1), zrow, -coi.transpose(0, 2, 1), zrow], axis=1).astype(BF16)
    dtile = jnp.tile(d_skip.astype(F32).reshape(g, 1, hs), (1, 1, nt))
    lpad = jnp.zeros((g, 1, LANES - p), F32)
    l16r = jnp.concatenate([l16r, lpad], axis=-1)
    l16i = jnp.concatenate([l16i, lpad], axis=-1)
    toep = _to_lane_order(_to_lane_order(toep, 1), 2)
    binj = _to_lane_order(binj, 1)
    ct = _to_lane_order(ct, 2)
    dtile = _to_lane_order(dtile, 2)

    per_g3 = lambda i: (i, 0, 0)
    return pl.pallas_call(
        functools.partial(_s5_kernel, streams=streams),
        grid=(g,),
        in_specs=[pl.BlockSpec((1, rows, nt * hs), per_g3),
                  pl.BlockSpec((1, nt * hs, nt * hs), per_g3),
                  pl.BlockSpec((1, nt * hs, 2 * LANES), per_g3),
                  pl.BlockSpec((1, 2 * LANES, nt * hs), per_g3),
                  pl.BlockSpec((1, 1, nt * hs), per_g3),
                  pl.BlockSpec((1, 1, LANES), per_g3),
                  pl.BlockSpec((1, 1, LANES), per_g3)],
        out_specs=pl.BlockSpec((1, rows, nt * hs), per_g3),
        out_shape=jax.ShapeDtypeStruct((g, rows, nt * hs), F32),
        scratch_shapes=[pltpu.VMEM((rows, LANES), F32)] * 4,
        compiler_params=_cparams("arbitrary"),
        name="s5_chunked_scan",
    )(ut, toep, binj, ct, dtile, l16r, l16i)


def _sb_kernel(q_ref, k_ref, v_ref, o_ref, acc_ref, run_ref, *, qb, kb, nsub):
    qi = pl.program_id(2)
    per_key = kb // qb
    pairs = nsub // per_key
    ri = lax.broadcasted_iota(jnp.int32, (kb, kb), 0)
    ci = lax.broadcasted_iota(jnp.int32, (kb, kb), 1)
    after = (ri > ci).astype(BF16)
    rq = lax.broadcasted_iota(jnp.int32, (qb, kb), 0)
    cq = lax.broadcasted_iota(jnp.int32, (qb, kb), 1)
    diag_valid = [cq < rq + r * qb for r in range(per_key)]
    base = pairs * qi

    def pre(sub, kblk, valid):
        s = _dot_nt(q_ref[0, sub * qb:(sub + 1) * qb, :], kblk)
        lse = jnp.log2(1.0 + jnp.exp2(-jnp.abs(s)))
        log_p = jnp.minimum(s, 0.0) - lse
        neg_log_1m = jnp.maximum(s, 0.0) + lse
        if valid is not None:
            neg_log_1m = jnp.where(valid, neg_log_1m, 0.0)
        hi, lo = _split2(neg_log_1m)
        later = _dot(hi, after) + _dot(lo, after)
        return log_p - later, jnp.sum(neg_log_1m, axis=1, keepdims=True)

    def post(sub, part, tot, vblk, valid, exists):
        rows = slice(sub * qb, (sub + 1) * qb)
        run = run_ref[rows, :]
        if exists is not None:
            run = run + jnp.where(exists, 0.0, SB_NO_KEYS)
            tot = jnp.where(exists, tot, 0.0)
        wts = jnp.exp2(part - run)
        if valid is not None:
            wts = jnp.where(valid, wts, 0.0)
        acc_ref[rows, :] += _dot(wts.astype(BF16), vblk)
        run_ref[rows, :] = run + tot

    def band_pre(e, diagonal):
        items = []
        for m in range(pairs):
            idx = base + m - e
            exists = None if diagonal else idx >= 0
            k0 = pl.multiple_of(jnp.maximum(idx, 0) * kb, kb)
            kblk, vblk = k_ref[0, pl.ds(k0, kb), :], v_ref[0, pl.ds(k0, kb), :]
            for r in range(per_key):
                sub = m * per_key + r
                valid = diag_valid[r] if diagonal else None
                items.append((sub, vblk, valid, exists, pre(sub, kblk, valid)))
        return items

    def band_post(items):
        for sub, vblk, valid, exists, (part, tot) in items:
            post(sub, part, tot, vblk, valid, exists)

    acc_ref[...] = jnp.zeros(acc_ref.shape, F32)
    run_ref[...] = jnp.zeros(run_ref.shape, F32)
    first = [band_pre(e, e == 0) for e in range(SB_STATIC_BANDS)]
    for items in first:
        band_post(items)

    last = base + pairs - 1

    def keep_going(state):
        e, run_min = state
        return jnp.logical_and(e <= last, run_min < SB_UNDERFLOW_LOG2)

    def body(state):
        e, _ = state
        band_post(band_pre(e, False))
        return e + 1, jnp.min(run_ref[...])

    lax.while_loop(keep_going, body, (jnp.int32(SB_STATIC_BANDS), jnp.min(run_ref[...])))
    o_ref[0] = acc_ref[...].astype(o_ref.dtype)


def _sb_attention(q, kv, heads):
    bsz, seq, _ = q.shape
    d = SB_HEAD_DIM
    kb = min(SB_KEY_BLOCK, seq)
    qb = min(SB_QUERY_BLOCK, kb)
    tq = min(SB_Q_TILE, seq)
    nsub = tq // qb
    return pl.pallas_call(
        functools.partial(_sb_kernel, qb=qb, kb=kb, nsub=nsub),
        grid=(bsz, heads, seq // tq),
        in_specs=[pl.BlockSpec((1, tq, d), lambda b, h, i: (b, i, h)),
                  pl.BlockSpec((1, seq, d), lambda b, h, i: (b, 0, h)),
                  pl.BlockSpec((1, seq, d), lambda b, h, i: (b, 0, heads + h))],
        out_specs=pl.BlockSpec((1, tq, d), lambda b, h, i: (b, i, h)),
        out_shape=jax.ShapeDtypeStruct((bsz, seq, heads * d), BF16),
        scratch_shapes=[pltpu.VMEM((tq, d), F32), pltpu.VMEM((tq, 1), F32)],
        compiler_params=_cparams("arbitrary", "arbitrary", "arbitrary"),
        name="stick_breaking_attention",
    )(q, kv, kv)


def _gelu_tanh(x):
    return 0.5 * x * (1.0 + jnp.tanh(0.7978845608028654 * (x + 0.044715 * (x * x * x))))


def _mix_ffn_kernel(*refs, even, final, f_chunks):
    if even:
        (h_ref, o_ref, y_ref, gw_ref, gb_ref, wtop_ref, wbot_ref, gtm_ref, nw_ref, sc_ref, sh_ref, gtf_ref,
         win_ref, wout_ref, fw_ref, out_ref, ybuf_ref) = refs
        y = _gelu_tanh(_from_chunk_major(y_ref, ybuf_ref))
        y = y * _sigmoid(_dot(y.astype(BF16), gw_ref[...]) + gb_ref[...])
        mix = _dot(o_ref[0], wtop_ref[...]) + _dot(y.astype(BF16), wbot_ref[...])
    else:
        (h_ref, o_ref, wtop_ref, gtm_ref, nw_ref, sc_ref, sh_ref, gtf_ref,
         win_ref, wout_ref, fw_ref, out_ref) = refs
        mix = _dot(o_ref[0], wtop_ref[...])
    h1 = h_ref[0] + gtm_ref[0] * mix
    hn = _norm_mod(h1, nw_ref[...], sc_ref[0], sh_ref[0]).astype(BF16)
    hidden = wout_ref.shape[0]
    fc = hidden // f_chunks
    acc = None
    for c in range(f_chunks):
        gate = _dot(hn, win_ref[:, c * fc:(c + 1) * fc])
        up = _dot(hn, win_ref[:, hidden + c * fc:hidden + (c + 1) * fc])
        part = _dot((_silu(gate) * up).astype(BF16), wout_ref[c * fc:(c + 1) * fc, :])
        acc = part if acc is None else acc + part
    h2 = h1 + gtf_ref[0] * acc
    if final:
        h2 = _rms(h2) * fw_ref[...]
    out_ref[0] = h2


def _mix_ffn(h, acts, mix_w, glu, gt_m, nw, sc, sh, gt_f, w_in, w_out, final_w, *, even, final):
    bsz, seq, d = h.shape
    tm = min(ROW_TILE, seq)
    hidden = w_out.shape[0]
    row = lambda b, t: (b, t, 0)
    per_b = lambda b, t: (b, 0, 0)
    vec_d = pl.BlockSpec((1, d), lambda b, t: (0, 0))
    mod_spec = pl.BlockSpec((1, 1, d), per_b)
    args, specs, scratch = [h], [pl.BlockSpec((1, tm, d), row)], []
    if even:
        o, y = acts
        half = o.shape[-1]
        nt = seq // tm
        glu_w, glu_b = glu
        args += [o, y, glu_w.astype(BF16), glu_b.astype(F32).reshape(1, -1),
                 mix_w[:half].astype(BF16), mix_w[half:].astype(BF16)]
        specs += [pl.BlockSpec((1, tm, half), row),
                  pl.BlockSpec((y.shape[0], tm // S5_CHUNK, y.shape[2]), lambda b, t: (0, b * nt + t, 0)),
                  _const_spec(glu_w.shape), pl.BlockSpec((1, glu_b.shape[-1]), lambda b, t: (0, 0)),
                  _const_spec((half, d)), _const_spec((d - half, d))]
        scratch = [pltpu.VMEM(((d - half) // LANES, tm, LANES), F32)]
    else:
        (o,) = acts
        args += [o, mix_w.astype(BF16)]
        specs += [pl.BlockSpec((1, tm, d), row), _const_spec((d, d))]
    args += [gt_m, nw.astype(F32).reshape(1, d), sc, sh, gt_f, w_in.astype(BF16), w_out.astype(BF16),
             final_w.astype(F32).reshape(1, d)]
    specs += [mod_spec, vec_d, mod_spec, mod_spec, mod_spec, _const_spec(w_in.shape), _const_spec(w_out.shape), vec_d]
    return pl.pallas_call(
        functools.partial(_mix_ffn_kernel, even=even, final=final, f_chunks=2),
        grid=(bsz, seq // tm),
        in_specs=specs,
        out_specs=pl.BlockSpec((1, tm, d), row),
        out_shape=jax.ShapeDtypeStruct((bsz, seq, d), F32),
        scratch_shapes=scratch,
        compiler_params=_cparams("arbitrary", "arbitrary"),
        name="outproj_swiglu_ffn",
    )(*args)


def kernel(x, c, ada_w, ada_b, norm_mix_w, norm_ffn_w, ffn_w_in, ffn_w_out, hy_w_in, hy_conv_w, hy_a_log, hy_dt_bias, hy_head_norm_w, s5_lam_re, s5_lam_im, s5_log_dt, s5_b_re, s5_b_im, s5_c_re, s5_c_im, s5_d, s5_glu_w, s5_glu_b, hy_w_out, sb_w_in, sb_w_out, final_norm_w):
    bsz, seq, d = x.shape
    depth = ada_w.shape[0]
    gdn_width = d // 2
    gdn_heads = gdn_width // GDN_HEAD_DIM
    s5_width = d - gdn_width
    sb_heads = d // SB_HEAD_DIM

    mod = _modulation(c, ada_w, ada_b)
    h = x
    for i in range(depth):
        sh_m, sc_m, gt_m, sh_f, sc_f, gt_f = (mod[i, :bsz, k * d:(k + 1) * d].reshape(bsz, 1, d) for k in range(6))
        j = i // 2
        last = i == depth - 1
        if i % 2 == 0:
            w = hy_w_in[j]
            w4 = 4 * gdn_width
            gate_cols = jnp.pad(w[:, w4:w4 + 2 * gdn_heads], ((0, 0), (0, LANES - 2 * gdn_heads)))
            w_cat = jnp.concatenate([w[:, :w4], gate_cols, w[:, w4 + 2 * gdn_heads:]], axis=1).astype(BF16)
            qkv_pre, zg, ba, ut = _inproj(h, norm_mix_w[i], sc_m, sh_m, w_cat,
                                          (3 * gdn_width, gdn_width, LANES, s5_width), (F32, F32, F32, F32),
                                          chunk_major_last=True)
            o = _gdn(qkv_pre, zg, ba, hy_conv_w[j], hy_a_log[j], hy_dt_bias[j], hy_head_norm_w[j])
            y = _s5(ut, bsz, s5_lam_re[j], s5_lam_im[j], s5_log_dt[j], s5_b_re[j], s5_b_im[j],
                    s5_c_re[j], s5_c_im[j], s5_d[j])
            h = _mix_ffn(h, (o, y), hy_w_out[j], (s5_glu_w[j], s5_glu_b[j]), gt_m, norm_ffn_w[i], sc_f, sh_f, gt_f,
                         ffn_w_in[i], ffn_w_out[i], final_norm_w, even=True, final=last)
        else:
            q, kv = _inproj(h, norm_mix_w[i], sc_m, sh_m, sb_w_in[j].astype(BF16), (d, 2 * d), (BF16, BF16),
                            scales=(SB_HEAD_DIM ** -0.5 * LOG2_E, 1.0))
            o = _sb_attention(q, kv, sb_heads)
            h = _mix_ffn(h, (o,), sb_w_out[j], None, gt_m, norm_ffn_w[i], sc_f, sh_f, gt_f,
                         ffn_w_in[i], ffn_w_out[i], final_norm_w, even=False, final=last)
    return h
```

```python
import functools

import jax
import jax.numpy as jnp
import numpy as np
from jax import lax
from jax.experimental import pallas as pl
from jax.experimental.pallas import tpu as pltpu

F32 = jnp.float32
BF16 = jnp.bfloat16
NORM_EPS = 1e-6

GDN_HEAD_DIM = 128
CONV_K = 4
GDN_CHUNK = 64
S5_GROUP = 16
S5_STATE = 64
SB_HEAD_DIM = 128

LANES = 128
SUBLANES = 8
VMEM_LIMIT_BYTES = 56 * 1024 * 1024

ROW_TILE = 512
GDN_TILE = 256
S5_CHUNK = 16
SB_KEY_BLOCK = 256
SB_QUERY_BLOCK = 128
SB_Q_TILE = 2048
SB_STATIC_BANDS = 2
SB_NO_KEYS = 1.0e6
LOG2_E = 1.4426950408889634
SB_UNDERFLOW_LOG2 = 160.0
MOD_COL_TILE = 1536


def _cparams(*sem):
    return pltpu.CompilerParams(dimension_semantics=sem, vmem_limit_bytes=VMEM_LIMIT_BYTES)


def _const_spec(shape):
    nd = len(shape)
    return pl.BlockSpec(shape, lambda *_: (0,) * nd, pipeline_mode=pl.Buffered(1))


def _dot(a, b):
    return jnp.dot(a, b, preferred_element_type=F32)


def _dot_nt(a, b):
    return lax.dot_general(a, b, (((1,), (1,)), ((), ())), preferred_element_type=F32)


def _dot_tn(a, b):
    return lax.dot_general(a, b, (((0,), (0,)), ((), ())), preferred_element_type=F32)


def _split2(x):
    hi = x.astype(BF16)
    lo = (x - hi.astype(F32)).astype(BF16)
    return hi, lo


def _mm3(a, b):
    ah, al = _split2(a)
    bh, bl = _split2(b)
    return _dot(ah, bh) + (_dot(ah, bl) + _dot(al, bh))


def _mm3_nt(a, b):
    ah, al = _split2(a)
    bh, bl = _split2(b)
    return _dot_nt(ah, bh) + (_dot_nt(ah, bl) + _dot_nt(al, bh))


def _sigmoid(x):
    return 1.0 / (1.0 + jnp.exp(-x))


def _silu(x):
    return x * _sigmoid(x)


def _softplus(x):
    return jnp.maximum(x, 0.0) + jnp.log(1.0 + jnp.exp(-jnp.abs(x)))


def _rms(x):
    return x * lax.rsqrt(jnp.mean(x * x, axis=-1, keepdims=True) + NORM_EPS)


def _norm_mod(x, nw, sc, sh):
    return (_rms(x) * nw) * (1.0 + sc) + sh


def _mod_kernel(c_ref, w_ref, b_ref, o_ref):
    ca = _silu(c_ref[...])
    o_ref[0] = _mm3(ca, w_ref[0]) + b_ref[0]


def _modulation(c, ada_w, ada_b):
    depth, d, n = ada_w.shape
    bsz = c.shape[0]
    rows = -(-bsz // SUBLANES) * SUBLANES
    c_pad = jnp.pad(c, ((0, rows - bsz), (0, 0)))
    tn = MOD_COL_TILE
    return pl.pallas_call(
        _mod_kernel,
        grid=(depth, n // tn),
        in_specs=[pl.BlockSpec((rows, d), lambda i, j: (0, 0)),
                  pl.BlockSpec((1, d, tn), lambda i, j: (i, 0, j)),
                  pl.BlockSpec((1, 1, tn), lambda i, j: (i, 0, j))],
        out_specs=pl.BlockSpec((1, rows, tn), lambda i, j: (i, 0, j)),
        out_shape=jax.ShapeDtypeStruct((depth, rows, n), F32),
        compiler_params=_cparams("arbitrary", "arbitrary"),
        name="adaln_modulation",
    )(c_pad, ada_w, ada_b.reshape(depth, 1, n))


def _s5_lane(group, s, hi=0):
    per_block = LANES // S5_GROUP
    return (s // per_block) * LANES + ((s + group) % per_block) * S5_GROUP + hi


def _to_lane_order(a, axis):
    per_block = LANES // S5_GROUP
    shp = a.shape
    a = a.reshape((shp[0] // per_block, per_block) + shp[1:axis]
                  + (S5_CHUNK // per_block, per_block, S5_GROUP) + shp[axis + 1:])
    parts = [jnp.roll(a[:, j], j, axis=axis + 1) for j in range(per_block)]
    return jnp.stack(parts, axis=1).reshape(shp)


def _to_chunk_major(x, z_ref, buf_ref):
    tokens, width = x.shape
    chunks = tokens // S5_CHUNK
    per_block = LANES // S5_GROUP
    for lb in range(width // LANES):
        buf_ref[lb] = x[:, lb * LANES:(lb + 1) * LANES]
    for s in range(S5_CHUNK):
        shift = (s % per_block) * S5_GROUP
        for lb in range(width // LANES):
            rows = buf_ref[lb, pl.ds(s, chunks, stride=S5_CHUNK), :]
            rows = pltpu.roll(rows, shift, axis=1) if shift else rows
            for j in range(per_block):
                src = ((s + j) % per_block) * S5_GROUP
                dst = _s5_lane(lb * per_block + j, s)
                z_ref[lb * per_block + j, :, dst:dst + S5_GROUP] = rows[:, src:src + S5_GROUP]


def _from_chunk_major(z_ref, buf_ref):
    groups, chunks, _ = z_ref.shape
    per_block = LANES // S5_GROUP
    nlb = groups // per_block
    lane_block = lax.broadcasted_iota(jnp.int32, (chunks, LANES), 1) // S5_GROUP
    for s in range(S5_CHUNK):
        half = s // per_block
        shift = (s % per_block) * S5_GROUP
        for lb in range(nlb):
            rows = None
            for j in range(per_block):
                piece = z_ref[lb * per_block + j, :, half * LANES:(half + 1) * LANES]
                rows = piece if rows is None else jnp.where(lane_block == (s + j) % per_block, piece, rows)
            rows = pltpu.roll(rows, LANES - shift, axis=1) if shift else rows
            buf_ref[lb, pl.ds(s, chunks, stride=S5_CHUNK), :] = rows
    return jnp.concatenate([buf_ref[lb] for lb in range(nlb)], axis=1)


def _inproj_kernel(h_ref, nw_ref, sc_ref, sh_ref, w_ref, *refs, widths, scales, chunk_major_last, head_dim):
    out_refs = refs[:len(widths)]
    hn = _norm_mod(h_ref[0], nw_ref[...], sc_ref[0], sh_ref[0]).astype(BF16)
    off = 0
    for idx, (o_ref, wd, s) in enumerate(zip(out_refs, widths, scales)):
        acc = _dot(hn, w_ref[:, off:off + wd])
        if s != 1.0:
            acc = acc * s
        if chunk_major_last and idx == len(widths) - 1:
            _to_chunk_major(acc, o_ref, refs[len(widths)])
        elif head_dim:
            for hh in range(wd // head_dim):
                o_ref[0, hh] = acc[:, hh * head_dim:(hh + 1) * head_dim].astype(o_ref.dtype)
        else:
            o_ref[0] = acc.astype(o_ref.dtype)
        off += wd


def _inproj(h, nw, sc, sh, w, widths, dtypes, scales=None, chunk_major_last=False, head_dim=0):
    scales = scales or (1.0,) * len(widths)
    bsz, seq, d = h.shape
    tm = min(ROW_TILE, seq)
    nt = seq // tm
    n = w.shape[1]
    row = lambda b, t: (b, t, 0)
    per_b = lambda b, t: (b, 0, 0)
    if head_dim:
        out_specs = [pl.BlockSpec((1, wd // head_dim, tm, head_dim), lambda b, t: (b, 0, t, 0)) for wd in widths]
        out_shape = [jax.ShapeDtypeStruct((bsz, wd // head_dim, seq, head_dim), dt) for wd, dt in zip(widths, dtypes)]
    else:
        out_specs = [pl.BlockSpec((1, tm, wd), row) for wd in widths]
        out_shape = [jax.ShapeDtypeStruct((bsz, seq, wd), dt) for wd, dt in zip(widths, dtypes)]
    scratch = []
    if chunk_major_last:
        groups = widths[-1] // S5_GROUP
        row_w = S5_CHUNK * S5_GROUP
        out_specs[-1] = pl.BlockSpec((groups, tm // S5_CHUNK, row_w), lambda b, t: (0, b * nt + t, 0))
        out_shape[-1] = jax.ShapeDtypeStruct((groups, bsz * seq // S5_CHUNK, row_w), dtypes[-1])
        scratch = [pltpu.VMEM((widths[-1] // LANES, tm, LANES), F32)]
    return pl.pallas_call(
        functools.partial(_inproj_kernel, widths=widths, scales=scales, chunk_major_last=chunk_major_last,
                          head_dim=head_dim),
        grid=(bsz, nt),
        in_specs=[pl.BlockSpec((1, tm, d), row),
                  pl.BlockSpec((1, d), lambda b, t: (0, 0)),
                  pl.BlockSpec((1, 1, d), per_b),
                  pl.BlockSpec((1, 1, d), per_b),
                  _const_spec((d, n))],
        out_specs=out_specs,
        out_shape=out_shape,
        scratch_shapes=scratch,
        compiler_params=_cparams("arbitrary", "arbitrary"),
        name="norm_mod_inproj",
    )(h, nw.reshape(1, d), sc, sh, w)


def _unit_lower_inverses(mats, ri, ci):
    n = mats[0].shape[0]
    eye = (ri == ci).astype(F32)
    pair = (ri // 2 == ci // 2) & (ri > ci)
    ts = [eye - jnp.where(pair, a, 0.0) for a in mats]
    s = 2
    while s < n:
        sel = (ri // (2 * s) == ci // (2 * s)) & (ri // s > ci // s)
        tmps = [_mm3(t, jnp.where(sel, a, 0.0)) for t, a in zip(ts, mats)]
        ts = [t - _mm3(tmp, t) for t, tmp in zip(ts, tmps)]
        s *= 2
    return ts


def _gdn_kernel(x_ref, z_ref, ba_ref, cw_ref, garow_ref, gbrow_ref, hnw_ref, o_ref,
                xbuf_ref, state_ref, *, heads, tile):
    dh = GDN_HEAD_DIM
    width = heads * dh
    ck = GDN_CHUNK
    nck = tile // ck
    t_id = pl.program_id(1)

    @pl.when(t_id == 0)
    def _():
        xbuf_ref[0:SUBLANES, :] = jnp.zeros((SUBLANES, 3 * width), F32)
        state_ref[...] = jnp.zeros(state_ref.shape, F32)

    x = x_ref[0]
    xbuf_ref[SUBLANES:SUBLANES + tile, :] = x
    conv = cw_ref[0:1, :] * xbuf_ref[pl.ds(SUBLANES - CONV_K + 1, tile), :]
    for k in range(1, CONV_K):
        conv = conv + cw_ref[k:k + 1, :] * xbuf_ref[pl.ds(SUBLANES - CONV_K + 1 + k, tile), :]
    xbuf_ref[0:SUBLANES, :] = x[tile - SUBLANES:tile, :]
    qkv = _silu(conv)

    ba = ba_ref[0]
    beta_all = _sigmoid(ba)
    g_all = -jnp.exp(garow_ref[...]) * _softplus(ba + gbrow_ref[...])
    ri_t = lax.broadcasted_iota(jnp.int32, (tile, tile), 0)
    ci_t = lax.broadcasted_iota(jnp.int32, (tile, tile), 1)
    cum_mat = ((ri_t // ck == ci_t // ck) & (ri_t >= ci_t)).astype(BF16)
    g_hi = g_all.astype(BF16)
    g_r1 = g_all - g_hi.astype(F32)
    g_mid = g_r1.astype(BF16)
    g_lo = (g_r1 - g_mid.astype(F32)).astype(BF16)
    gc_all = _dot(cum_mat, g_hi) + (_dot(cum_mat, g_mid) + _dot(cum_mat, g_lo))
    gc_all_t = gc_all.T

    ri = lax.broadcasted_iota(jnp.int32, (ck, ck), 0)
    ci = lax.broadcasted_iota(jnp.int32, (ck, ck), 1)
    hnw = hnw_ref[...]
    z_all = z_ref[0]

    prep = []
    for n in range(nck):
        sl = slice(n * ck, (n + 1) * ck)
        for h in range(heads):
            q = qkv[sl, h * dh:(h + 1) * dh]
            k = qkv[sl, width + h * dh:width + (h + 1) * dh]
            v = qkv[sl, 2 * width + h * dh:2 * width + (h + 1) * dh]
            q = q * lax.rsqrt(jnp.sum(q * q, axis=-1, keepdims=True) + NORM_EPS) * (dh ** -0.5)
            k = k * lax.rsqrt(jnp.sum(k * k, axis=-1, keepdims=True) + NORM_EPS)
            beta = beta_all[sl, h:h + 1]
            gcc = gc_all[sl, heads + h:heads + h + 1]
            gcr = gc_all_t[heads + h:heads + h + 1, sl]
            g_last = gcc[ck - 1:ck, :]
            decay = jnp.where(ri >= ci, jnp.exp(jnp.minimum(gcc - gcr, 0.0)), 0.0)
            k_beta = k * beta
            kb16, k16 = k_beta.astype(BF16), k.astype(BF16)
            egc = jnp.exp(gcc)
            prep.append(dict(
                a=jnp.where(ri > ci, _dot_nt(kb16, k16) * decay, 0.0),
                rhs=jnp.concatenate([v * beta, k_beta * egc], axis=-1),
                attn=(_dot_nt(q.astype(BF16), k16) * decay).astype(BF16),
                q_dec=(q * egc).astype(BF16),
                k_dec=(k * jnp.exp(g_last - gcc)).astype(BF16),
                s_decay=jnp.exp(g_last)))
    invs = _unit_lower_inverses([p["a"] for p in prep], ri, ci)
    sols = [_mm3(t, p["rhs"]) for t, p in zip(invs, prep)]

    states = [state_ref[h] for h in range(heads)]
    outs = [[] for _ in range(heads)]
    for n in range(nck):
        cur = [(prep[n * heads + h], sols[n * heads + h]) for h in range(heads)]
        s16 = [st.astype(BF16) for st in states]
        vn16 = [(sol[:, :dh] - _dot(sol[:, dh:].astype(BF16), s)).astype(BF16) for (_, sol), s in zip(cur, s16)]
        for h in range(heads):
            p = cur[h][0]
            outs[h].append(_dot(p["q_dec"], s16[h]) + _dot(p["attn"], vn16[h]))
        states = [st * cur[h][0]["s_decay"] + _dot_tn(cur[h][0]["k_dec"], vn16[h]) for h, st in enumerate(states)]
    for h in range(heads):
        state_ref[h] = states[h]
        o_h = jnp.concatenate(outs[h], axis=0) if nck > 1 else outs[h][0]
        o_h = (_rms(o_h) * hnw) * _silu(z_all[:, h * dh:(h + 1) * dh])
        o_ref[0, :, h * dh:(h + 1) * dh] = o_h.astype(o_ref.dtype)


def _gdn(qkv_pre, z, ba, conv_w, a_log, dt_bias, head_norm_w):
    bsz, seq, w3 = qkv_pre.shape
    width = w3 // 3
    heads = width // GDN_HEAD_DIM
    tile = min(GDN_TILE, seq)
    garow = jnp.zeros((1, LANES), F32).at[0, heads:2 * heads].set(a_log.astype(F32))
    gbrow = jnp.zeros((1, LANES), F32).at[0, heads:2 * heads].set(dt_bias.astype(F32))
    row = lambda b, t: (b, t, 0)
    fix2 = lambda b, t: (0, 0)
    return pl.pallas_call(
        functools.partial(_gdn_kernel, heads=heads, tile=tile),
        grid=(bsz, seq // tile),
        in_specs=[pl.BlockSpec((1, tile, w3), row),
                  pl.BlockSpec((1, tile, width), row),
                  pl.BlockSpec((1, tile, LANES), row),
                  pl.BlockSpec((CONV_K, w3), fix2),
                  pl.BlockSpec((1, LANES), fix2),
                  pl.BlockSpec((1, LANES), fix2),
                  pl.BlockSpec((1, GDN_HEAD_DIM), fix2)],
        out_specs=pl.BlockSpec((1, tile, width), row),
        out_shape=jax.ShapeDtypeStruct((bsz, seq, width), BF16),
        scratch_shapes=[pltpu.VMEM((SUBLANES + tile, w3), F32),
                        pltpu.VMEM((heads, GDN_HEAD_DIM, GDN_HEAD_DIM), F32)],
        compiler_params=_cparams("arbitrary", "arbitrary"),
        name="gated_deltanet",
    )(qkv_pre, z, ba, conv_w.astype(F32), garow, gbrow, head_norm_w.astype(F32).reshape(1, -1))


def _s5_prep_kernel(lr_ref, li_ref, ldt_ref, btr_ref, bti_ref, cr_ref, ci_ref,
                    kst_ref, bjr_ref, bji_ref, cor_ref, coi_ref, l16r_ref, l16i_ref):
    nt = S5_CHUNK
    lr, li = lr_ref[0], li_ref[0]
    dt = jnp.exp(ldt_ref[0])
    mag = jnp.exp(lr * dt)
    lb_re, lb_im = mag * jnp.cos(li * dt), mag * jnp.sin(li * dt)
    den = lr * lr + li * li
    num_re, num_im = lb_re - 1.0, lb_im
    f_re = (num_re * lr + num_im * li) / den
    f_im = (num_im * lr - num_re * li) / den
    btr, bti = btr_ref[0], bti_ref[0]
    bb_re = f_re * btr - f_im * bti
    bb_im = f_re * bti + f_im * btr
    cr, ci = cr_ref[0], ci_ref[0]
    p_re, p_im = [jnp.ones_like(lr)], [jnp.zeros_like(lr)]
    for _ in range(nt):
        p_re.append(p_re[-1] * lb_re - p_im[-1] * lb_im)
        p_im.append(p_re[-2] * lb_im + p_im[-1] * lb_re)
    cp_re = [cr * pr - ci * pi for pr, pi in zip(p_re, p_im)]
    cp_im = [cr * pi + ci * pr for pr, pi in zip(p_re, p_im)]
    kst_ref[0] = (_mm3_nt(jnp.concatenate(cp_re[:nt], axis=0), bb_re)
                  - _mm3_nt(jnp.concatenate(cp_im[:nt], axis=0), bb_im))
    bjr_ref[0] = jnp.concatenate([bb_re * p_re[nt - 1 - s] - bb_im * p_im[nt - 1 - s] for s in range(nt)], axis=0)
    bji_ref[0] = jnp.concatenate([bb_re * p_im[nt - 1 - s] + bb_im * p_re[nt - 1 - s] for s in range(nt)], axis=0)
    cor_ref[0] = jnp.concatenate(cp_re[1:nt + 1], axis=0)
    coi_ref[0] = jnp.concatenate(cp_im[1:nt + 1], axis=0)
    l16r_ref[0] = p_re[nt]
    l16i_ref[0] = p_im[nt]


def _s5_prep(lam_re, lam_im, log_dt, b_re, b_im, c_re, c_im):
    g, p = lam_re.shape
    hs = c_re.shape[1]
    nt = S5_CHUNK
    vec = pl.BlockSpec((1, 1, p), lambda i: (i, 0, 0))
    mat = pl.BlockSpec((1, hs, p), lambda i: (i, 0, 0))
    big = pl.BlockSpec((1, nt * hs, p), lambda i: (i, 0, 0))
    f32 = lambda a: a.astype(F32)
    return pl.pallas_call(
        _s5_prep_kernel,
        grid=(g,),
        in_specs=[vec, vec, pl.BlockSpec((1, 1, 1), lambda i: (i, 0, 0)), mat, mat, mat, mat],
        out_specs=[pl.BlockSpec((1, nt * hs, hs), lambda i: (i, 0, 0)), big, big, big, big, vec, vec],
        out_shape=[jax.ShapeDtypeStruct((g, nt * hs, hs), F32)]
                  + [jax.ShapeDtypeStruct((g, nt * hs, p), F32)] * 4
                  + [jax.ShapeDtypeStruct((g, 1, p), F32)] * 2,
        compiler_params=_cparams("arbitrary"),
        name="s5_discretise",
    )(f32(lam_re).reshape(g, 1, p), f32(lam_im).reshape(g, 1, p), f32(log_dt).reshape(g, 1, 1),
      f32(b_re).transpose(0, 2, 1), f32(b_im).transpose(0, 2, 1), f32(c_re), f32(c_im))


def _s5_kernel(u_ref, toep_ref, binj_ref, ct_ref, dt_ref, l16r_ref, l16i_ref, y_ref,
               sre_ref, sim_ref, xre_ref, xim_ref, *, streams):
    rows = u_ref.shape[1]
    steps = rows // streams
    u = u_ref[0]
    ub = u.astype(BF16)
    inj = _dot(ub, binj_ref[0])
    sre_ref[...] = inj[:, :LANES]
    sim_ref[...] = inj[:, LANES:]
    lr = jnp.broadcast_to(l16r_ref[0], (streams, LANES))
    li = jnp.broadcast_to(l16i_ref[0], (streams, LANES))

    def scan_step(i, carry):
        xr, xi = carry
        at = pl.ds(i, streams, stride=steps)
        xre_ref[at, :] = xr
        xim_ref[at, :] = xi
        sr = sre_ref[at, :]
        si = sim_ref[at, :]
        return lr * xr - li * xi + sr, lr * xi + li * xr + si

    zero = jnp.zeros((streams, LANES), F32)
    lax.fori_loop(0, steps, scan_step, (zero, zero), unroll=8)

    xin = jnp.concatenate([xre_ref[...], xim_ref[...]], axis=1).astype(BF16)
    y_ref[0] = _dot(ub, toep_ref[0]) + _dot(xin, ct_ref[0]) + dt_ref[0] * u


def _s5(ut, bsz, lam_re, lam_im, log_dt, b_re, b_im, c_re, c_im, d_skip):
    g, rows, _ = ut.shape
    hs, p, nt = S5_GROUP, S5_STATE, S5_CHUNK
    streams = bsz
    assert p * 2 == LANES and nt * hs == 2 * LANES
    steps = rows // streams
    kst, bjr, bji, cor, coi, l16r, l16i = _s5_prep(lam_re, lam_im, log_dt, b_re, b_im, c_re, c_im)

    k4 = kst.reshape(g, nt, hs, hs)
    toep = jnp.stack([jnp.pad(k4[:, :nt - s], ((0, 0), (s, 0), (0, 0), (0, 0))) for s in range(nt)],
                     axis=1)
    toep = toep.transpose(0, 1, 4, 2, 3).reshape(g, nt * hs, nt * hs).astype(BF16)
    zpad = jnp.zeros((g, nt * hs, LANES - p), F32)
    binj = jnp.concatenate([bjr, zpad, bji, zpad], axis=-1).astype(BF16)
    zrow = jnp.zeros((g, LANES - p, nt * hs), F32)
    ct = jnp.concatenate([cor.transpose(0, 2, 1), zrow, -coi.transpose(0, 2, 1), zrow], axis=1).astype(BF16)
    dtile = jnp.tile(d_skip.astype(F32).reshape(g, 1, hs), (1, 1, nt))
    lpad = jnp.zeros((g, 1, LANES - p), F32)
    l16r = jnp.concatenate([l16r, lpad], axis=-1)
    l16i = jnp.concatenate([l16i, lpad], axis=-1)
    toep = _to_lane_order(_to_lane_order(toep, 1), 2)
    binj = _to_lane_order(binj, 1)
    ct = _to_lane_order(ct, 2)
    dtile = _to_lane_order(dtile, 2)

    per_g3 = lambda i: (i, 0, 0)
    return pl.pallas_call(
        functools.partial(_s5_kernel, streams=streams),
        grid=(g,),
        in_specs=[pl.BlockSpec((1, rows, nt * hs), per_g3),
                  pl.BlockSpec((1, nt * hs, nt * hs), per_g3),
                  pl.BlockSpec((1, nt * hs, 2 * LANES), per_g3),
                  pl.BlockSpec((1, 2 * LANES, nt * hs), per_g3),
                  pl.BlockSpec((1, 1, nt * hs), per_g3),
                  pl.BlockSpec((1, 1, LANES), per_g3),
                  pl.BlockSpec((1, 1, LANES), per_g3)],
        out_specs=pl.BlockSpec((1, rows, nt * hs), per_g3),
        out_shape=jax.ShapeDtypeStruct((g, rows, nt * hs), F32),
        scratch_shapes=[pltpu.VMEM((rows, LANES), F32)] * 4,
        compiler_params=_cparams("arbitrary"),
        name="s5_chunked_scan",
    )(ut, toep, binj, ct, dtile, l16r, l16i)


def _sb_kernel(q_ref, k_ref, v_ref, o_ref, acc_ref, run_ref, *, qb, kb, nsub):
    qi = pl.program_id(2)
    per_key = kb // qb
    pairs = nsub // per_key
    ri = lax.broadcasted_iota(jnp.int32, (kb, kb), 0)
    ci = lax.broadcasted_iota(jnp.int32, (kb, kb), 1)
    after = (ri > ci).astype(BF16)
    rq = lax.broadcasted_iota(jnp.int32, (qb, kb), 0)
    cq = lax.broadcasted_iota(jnp.int32, (qb, kb), 1)
    diag_valid = [cq < rq + r * qb for r in range(per_key)]
    base = pairs * qi

    def pre(sub, kblk, valid):
        s = _dot_nt(q_ref[0, 0, sub * qb:(sub + 1) * qb, :], kblk)
        lse = jnp.log2(1.0 + jnp.exp2(-jnp.abs(s)))
        log_p = jnp.minimum(s, 0.0) - lse
        neg_log_1m = jnp.maximum(s, 0.0) + lse
        if valid is not None:
            neg_log_1m = jnp.where(valid, neg_log_1m, 0.0)
        hi, lo = _split2(neg_log_1m)
        later = _dot(hi, after) + _dot(lo, after)
        return log_p - later, jnp.sum(neg_log_1m, axis=1, keepdims=True)

    def post(sub, part, tot, vblk, valid, exists):
        rows = slice(sub * qb, (sub + 1) * qb)
        run = run_ref[rows, :]
        if exists is not None:
            run = run + jnp.where(exists, 0.0, SB_NO_KEYS)
            tot = jnp.where(exists, tot, 0.0)
        wts = jnp.exp2(part - run)
        if valid is not None:
            wts = jnp.where(valid, wts, 0.0)
        acc_ref[rows, :] += _dot(wts.astype(BF16), vblk)
        run_ref[rows, :] = run + tot

    def band_pre(e, diagonal):
        items = []
        for m in range(pairs):
            idx = base + m - e
            exists = None if diagonal else idx >= 0
            k0 = pl.multiple_of(jnp.maximum(idx, 0) * kb, kb)
            kblk, vblk = k_ref[0, 0, pl.ds(k0, kb), :], v_ref[0, 0, pl.ds(k0, kb), :]
            for r in range(per_key):
                sub = m * per_key + r
                valid = diag_valid[r] if diagonal else None
                items.append((sub, vblk, valid, exists, pre(sub, kblk, valid)))
        return items

    def band_post(items):
        for sub, vblk, valid, exists, (part, tot) in items:
            post(sub, part, tot, vblk, valid, exists)

    acc_ref[...] = jnp.zeros(acc_ref.shape, F32)
    run_ref[...] = jnp.zeros(run_ref.shape, F32)
    first = [band_pre(e, e == 0) for e in range(SB_STATIC_BANDS)]
    for items in first:
        band_post(items)

    last = base + pairs - 1

    def keep_going(state):
        e, run_min = state
        return jnp.logical_and(e <= last, run_min < SB_UNDERFLOW_LOG2)

    def body(state):
        e, _ = state
        band_post(band_pre(e, False))
        return e + 1, jnp.min(run_ref[...])

    lax.while_loop(keep_going, body, (jnp.int32(SB_STATIC_BANDS), jnp.min(run_ref[...])))
    o_ref[0, 0] = acc_ref[...].astype(o_ref.dtype)


def _sb_attention(q, kv):
    bsz, heads, seq, d = q.shape
    kb = min(SB_KEY_BLOCK, seq)
    qb = min(SB_QUERY_BLOCK, kb)
    tq = min(SB_Q_TILE, seq)
    nsub = tq // qb
    return pl.pallas_call(
        functools.partial(_sb_kernel, qb=qb, kb=kb, nsub=nsub),
        grid=(bsz, heads, seq // tq),
        in_specs=[pl.BlockSpec((1, 1, tq, d), lambda b, h, i: (b, h, i, 0)),
                  pl.BlockSpec((1, 1, seq, d), lambda b, h, i: (b, h, 0, 0)),
                  pl.BlockSpec((1, 1, seq, d), lambda b, h, i: (b, heads + h, 0, 0))],
        out_specs=pl.BlockSpec((1, 1, tq, d), lambda b, h, i: (b, h, i, 0)),
        out_shape=jax.ShapeDtypeStruct((bsz, heads, seq, d), BF16),
        scratch_shapes=[pltpu.VMEM((tq, d), F32), pltpu.VMEM((tq, 1), F32)],
        compiler_params=_cparams("arbitrary", "arbitrary", "arbitrary"),
        name="stick_breaking_attention",
    )(q, kv, kv)


def _gelu_tanh(x):
    return 0.5 * x * (1.0 + jnp.tanh(0.7978845608028654 * (x + 0.044715 * (x * x * x))))


def _mix_ffn_kernel(*refs, even, final, f_chunks):
    if even:
        (h_ref, o_ref, y_ref, gw_ref, gb_ref, wtop_ref, wbot_ref, gtm_ref, nw_ref, sc_ref, sh_ref, gtf_ref,
         win_ref, wout_ref, fw_ref, out_ref, ybuf_ref) = refs
        y = _gelu_tanh(_from_chunk_major(y_ref, ybuf_ref))
        y = y * _sigmoid(_dot(y.astype(BF16), gw_ref[...]) + gb_ref[...])
        mix = _dot(o_ref[0], wtop_ref[...]) + _dot(y.astype(BF16), wbot_ref[...])
    else:
        (h_ref, o_ref, wtop_ref, gtm_ref, nw_ref, sc_ref, sh_ref, gtf_ref,
         win_ref, wout_ref, fw_ref, out_ref) = refs
        o = jnp.concatenate([o_ref[0, hh] for hh in range(o_ref.shape[1])], axis=1)
        mix = _dot(o, wtop_ref[...])
    h1 = h_ref[0] + gtm_ref[0] * mix
    hn = _norm_mod(h1, nw_ref[...], sc_ref[0], sh_ref[0]).astype(BF16)
    hidden = wout_ref.shape[0]
    fc = hidden // f_chunks
    acc = None
    for c in range(f_chunks):
        gate = _dot(hn, win_ref[:, c * fc:(c + 1) * fc])
        up = _dot(hn, win_ref[:, hidden + c * fc:hidden + (c + 1) * fc])
        part = _dot((_silu(gate) * up).astype(BF16), wout_ref[c * fc:(c + 1) * fc, :])
        acc = part if acc is None else acc + part
    h2 = h1 + gtf_ref[0] * acc
    if final:
        h2 = _rms(h2) * fw_ref[...]
    out_ref[0] = h2


def _mix_ffn(h, acts, mix_w, glu, gt_m, nw, sc, sh, gt_f, w_in, w_out, final_w, *, even, final):
    bsz, seq, d = h.shape
    tm = min(ROW_TILE, seq)
    hidden = w_out.shape[0]
    row = lambda b, t: (b, t, 0)
    per_b = lambda b, t: (b, 0, 0)
    vec_d = pl.BlockSpec((1, d), lambda b, t: (0, 0))
    mod_spec = pl.BlockSpec((1, 1, d), per_b)
    args, specs, scratch = [h], [pl.BlockSpec((1, tm, d), row)], []
    if even:
        o, y = acts
        half = o.shape[-1]
        nt = seq // tm
        glu_w, glu_b = glu
        args += [o, y, glu_w.astype(BF16), glu_b.astype(F32).reshape(1, -1),
                 mix_w[:half].astype(BF16), mix_w[half:].astype(BF16)]
        specs += [pl.BlockSpec((1, tm, half), row),
                  pl.BlockSpec((y.shape[0], tm // S5_CHUNK, y.shape[2]), lambda b, t: (0, b * nt + t, 0)),
                  _const_spec(glu_w.shape), pl.BlockSpec((1, glu_b.shape[-1]), lambda b, t: (0, 0)),
                  _const_spec((half, d)), _const_spec((d - half, d))]
        scratch = [pltpu.VMEM(((d - half) // LANES, tm, LANES), F32)]
    else:
        (o,) = acts
        args += [o, mix_w.astype(BF16)]
        specs += [pl.BlockSpec((1, o.shape[1], tm, o.shape[3]), lambda b, t: (b, 0, t, 0)), _const_spec((d, d))]
    args += [gt_m, nw.astype(F32).reshape(1, d), sc, sh, gt_f, w_in.astype(BF16), w_out.astype(BF16),
             final_w.astype(F32).reshape(1, d)]
    specs += [mod_spec, vec_d, mod_spec, mod_spec, mod_spec, _const_spec(w_in.shape), _const_spec(w_out.shape), vec_d]
    return pl.pallas_call(
        functools.partial(_mix_ffn_kernel, even=even, final=final, f_chunks=2),
        grid=(bsz, seq // tm),
        in_specs=specs,
        out_specs=pl.BlockSpec((1, tm, d), row),
        out_shape=jax.ShapeDtypeStruct((bsz, seq, d), F32),
        scratch_shapes=scratch,
        compiler_params=_cparams("arbitrary", "arbitrary"),
        name="outproj_swiglu_ffn",
    )(*args)


def kernel(x, c, ada_w, ada_b, norm_mix_w, norm_ffn_w, ffn_w_in, ffn_w_out, hy_w_in, hy_conv_w, hy_a_log, hy_dt_bias, hy_head_norm_w, s5_lam_re, s5_lam_im, s5_log_dt, s5_b_re, s5_b_im, s5_c_re, s5_c_im, s5_d, s5_glu_w, s5_glu_b, hy_w_out, sb_w_in, sb_w_out, final_norm_w):
    bsz, seq, d = x.shape
    depth = ada_w.shape[0]
    gdn_width = d // 2
    gdn_heads = gdn_width // GDN_HEAD_DIM
    s5_width = d - gdn_width
    sb_heads = d // SB_HEAD_DIM

    mod = _modulation(c, ada_w, ada_b)
    h = x
    for i in range(depth):
        sh_m, sc_m, gt_m, sh_f, sc_f, gt_f = (mod[i, :bsz, k * d:(k + 1) * d].reshape(bsz, 1, d) for k in range(6))
        j = i // 2
        last = i == depth - 1
        if i % 2 == 0:
            w = hy_w_in[j]
            w4 = 4 * gdn_width
            gate_cols = jnp.pad(w[:, w4:w4 + 2 * gdn_heads], ((0, 0), (0, LANES - 2 * gdn_heads)))
            w_cat = jnp.concatenate([w[:, :w4], gate_cols, w[:, w4 + 2 * gdn_heads:]], axis=1).astype(BF16)
            qkv_pre, zg, ba, ut = _inproj(h, norm_mix_w[i], sc_m, sh_m, w_cat,
                                          (3 * gdn_width, gdn_width, LANES, s5_width), (F32, F32, F32, F32),
                                          chunk_major_last=True)
            o = _gdn(qkv_pre, zg, ba, hy_conv_w[j], hy_a_log[j], hy_dt_bias[j], hy_head_norm_w[j])
            y = _s5(ut, bsz, s5_lam_re[j], s5_lam_im[j], s5_log_dt[j], s5_b_re[j], s5_b_im[j],
                    s5_c_re[j], s5_c_im[j], s5_d[j])
            h = _mix_ffn(h, (o, y), hy_w_out[j], (s5_glu_w[j], s5_glu_b[j]), gt_m, norm_ffn_w[i], sc_f, sh_f, gt_f,
                         ffn_w_in[i], ffn_w_out[i], final_norm_w, even=True, final=last)
        else:
            q, kv = _inproj(h, norm_mix_w[i], sc_m, sh_m, sb_w_in[j].astype(BF16), (d, 2 * d), (BF16, BF16),
                            scales=(SB_HEAD_DIM ** -0.5 * LOG2_E, 1.0), head_dim=SB_HEAD_DIM)
            o = _sb_attention(q, kv)
            h = _mix_ffn(h, (o,), sb_w_out[j], None, gt_m, norm_ffn_w[i], sc_f, sh_f, gt_f,
                         ffn_w_in[i], ffn_w_out[i], final_norm_w, even=False, final=last)
    return h
```

```python
import functools

import jax
import jax.numpy as jnp
import numpy as np
from jax import lax
from jax.experimental import pallas as pl
from jax.experimental.pallas import tpu as pltpu

F32 = jnp.float32
BF16 = jnp.bfloat16
NORM_EPS = 1e-6

GDN_HEAD_DIM = 128
CONV_K = 4
GDN_CHUNK = 64
S5_GROUP = 16
S5_STATE = 64
SB_HEAD_DIM = 128

LANES = 128
SUBLANES = 8
VMEM_LIMIT_BYTES = 56 * 1024 * 1024

ROW_TILE = 512
GDN_TILE = 256
S5_CHUNK = 16
SB_KEY_BLOCK = 256
SB_QUERY_BLOCK = 128
SB_Q_TILE = 2048
SB_STATIC_BANDS = 2
SB_NO_KEYS = 1.0e6
LOG2_E = 1.4426950408889634
SB_UNDERFLOW_LOG2 = 160.0
MOD_COL_TILE = 1536


def _cparams(*sem):
    return pltpu.CompilerParams(dimension_semantics=sem, vmem_limit_bytes=VMEM_LIMIT_BYTES)


def _const_spec(shape):
    nd = len(shape)
    return pl.BlockSpec(shape, lambda *_: (0,) * nd, pipeline_mode=pl.Buffered(1))


def _dot(a, b):
    return jnp.dot(a, b, preferred_element_type=F32)


def _dot_nt(a, b):
    return lax.dot_general(a, b, (((1,), (1,)), ((), ())), preferred_element_type=F32)


def _dot_tn(a, b):
    return lax.dot_general(a, b, (((0,), (0,)), ((), ())), preferred_element_type=F32)


def _split2(x):
    hi = x.astype(BF16)
    lo = (x - hi.astype(F32)).astype(BF16)
    return hi, lo


def _mm3(a, b):
    ah, al = _split2(a)
    bh, bl = _split2(b)
    return _dot(ah, bh) + (_dot(ah, bl) + _dot(al, bh))


def _mm3_nt(a, b):
    ah, al = _split2(a)
    bh, bl = _split2(b)
    return _dot_nt(ah, bh) + (_dot_nt(ah, bl) + _dot_nt(al, bh))


def _sigmoid(x):
    return 1.0 / (1.0 + jnp.exp(-x))


def _silu(x):
    return x * _sigmoid(x)


def _softplus(x):
    return jnp.maximum(x, 0.0) + jnp.log(1.0 + jnp.exp(-jnp.abs(x)))


def _rms(x):
    return x * lax.rsqrt(jnp.mean(x * x, axis=-1, keepdims=True) + NORM_EPS)


def _norm_mod(x, nw, sc, sh):
    return (_rms(x) * nw) * (1.0 + sc) + sh


def _mod_kernel(c_ref, w_ref, b_ref, o_ref):
    ca = _silu(c_ref[...])
    o_ref[0] = _mm3(ca, w_ref[0]) + b_ref[0]


def _modulation(c, ada_w, ada_b):
    depth, d, n = ada_w.shape
    bsz = c.shape[0]
    rows = -(-bsz // SUBLANES) * SUBLANES
    c_pad = jnp.pad(c, ((0, rows - bsz), (0, 0)))
    tn = MOD_COL_TILE
    return pl.pallas_call(
        _mod_kernel,
        grid=(depth, n // tn),
        in_specs=[pl.BlockSpec((rows, d), lambda i, j: (0, 0)),
                  pl.BlockSpec((1, d, tn), lambda i, j: (i, 0, j)),
                  pl.BlockSpec((1, 1, tn), lambda i, j: (i, 0, j))],
        out_specs=pl.BlockSpec((1, rows, tn), lambda i, j: (i, 0, j)),
        out_shape=jax.ShapeDtypeStruct((depth, rows, n), F32),
        compiler_params=_cparams("arbitrary", "arbitrary"),
        name="adaln_modulation",
    )(c_pad, ada_w, ada_b.reshape(depth, 1, n))


def _s5_lane(group, s, hi=0):
    per_block = LANES // S5_GROUP
    return (s // per_block) * LANES + ((s + group) % per_block) * S5_GROUP + hi


def _to_chunk_major(x, z_ref, buf_ref):
    tokens, width = x.shape
    chunks = tokens // S5_CHUNK
    per_block = LANES // S5_GROUP
    for lb in range(width // LANES):
        buf_ref[lb] = x[:, lb * LANES:(lb + 1) * LANES]
    for s in range(S5_CHUNK):
        shift = (s % per_block) * S5_GROUP
        for lb in range(width // LANES):
            rows = buf_ref[lb, pl.ds(s, chunks, stride=S5_CHUNK), :]
            rows = pltpu.roll(rows, shift, axis=1) if shift else rows
            for j in range(per_block):
                src = ((s + j) % per_block) * S5_GROUP
                dst = _s5_lane(lb * per_block + j, s)
                z_ref[lb * per_block + j, :, dst:dst + S5_GROUP] = rows[:, src:src + S5_GROUP]


def _from_chunk_major(z_ref, buf_ref):
    groups, chunks, _ = z_ref.shape
    per_block = LANES // S5_GROUP
    nlb = groups // per_block
    lane_block = lax.broadcasted_iota(jnp.int32, (chunks, LANES), 1) // S5_GROUP
    for s in range(S5_CHUNK):
        half = s // per_block
        shift = (s % per_block) * S5_GROUP
        for lb in range(nlb):
            rows = None
            for j in range(per_block):
                piece = z_ref[lb * per_block + j, :, half * LANES:(half + 1) * LANES]
                rows = piece if rows is None else jnp.where(lane_block == (s + j) % per_block, piece, rows)
            rows = pltpu.roll(rows, LANES - shift, axis=1) if shift else rows
            buf_ref[lb, pl.ds(s, chunks, stride=S5_CHUNK), :] = rows
    return jnp.concatenate([buf_ref[lb] for lb in range(nlb)], axis=1)


def _inproj_kernel(h_ref, nw_ref, sc_ref, sh_ref, w_ref, *refs, widths, scales, chunk_major_last, head_dim):
    out_refs = refs[:len(widths)]
    hn = _norm_mod(h_ref[0], nw_ref[...], sc_ref[0], sh_ref[0]).astype(BF16)
    off = 0
    for idx, (o_ref, wd, s) in enumerate(zip(out_refs, widths, scales)):
        acc = _dot(hn, w_ref[:, off:off + wd])
        if s != 1.0:
            acc = acc * s
        if chunk_major_last and idx == len(widths) - 1:
            _to_chunk_major(acc, o_ref, refs[len(widths)])
        elif head_dim:
            for hh in range(wd // head_dim):
                o_ref[0, hh] = acc[:, hh * head_dim:(hh + 1) * head_dim].astype(o_ref.dtype)
        else:
            o_ref[0] = acc.astype(o_ref.dtype)
        off += wd


def _inproj(h, nw, sc, sh, w, widths, dtypes, scales=None, chunk_major_last=False, head_dim=0):
    scales = scales or (1.0,) * len(widths)
    bsz, seq, d = h.shape
    tm = min(ROW_TILE, seq)
    nt = seq // tm
    n = w.shape[1]
    row = lambda b, t: (b, t, 0)
    per_b = lambda b, t: (b, 0, 0)
    if head_dim:
        out_specs = [pl.BlockSpec((1, wd // head_dim, tm, head_dim), lambda b, t: (b, 0, t, 0)) for wd in widths]
        out_shape = [jax.ShapeDtypeStruct((bsz, wd // head_dim, seq, head_dim), dt) for wd, dt in zip(widths, dtypes)]
    else:
        out_specs = [pl.BlockSpec((1, tm, wd), row) for wd in widths]
        out_shape = [jax.ShapeDtypeStruct((bsz, seq, wd), dt) for wd, dt in zip(widths, dtypes)]
    scratch = []
    if chunk_major_last:
        groups = widths[-1] // S5_GROUP
        row_w = S5_CHUNK * S5_GROUP
        out_specs[-1] = pl.BlockSpec((groups, tm // S5_CHUNK, row_w), lambda b, t: (0, b * nt + t, 0))
        out_shape[-1] = jax.ShapeDtypeStruct((groups, bsz * seq // S5_CHUNK, row_w), dtypes[-1])
        scratch = [pltpu.VMEM((widths[-1] // LANES, tm, LANES), F32)]
    return pl.pallas_call(
        functools.partial(_inproj_kernel, widths=widths, scales=scales, chunk_major_last=chunk_major_last,
                          head_dim=head_dim),
        grid=(bsz, nt),
        in_specs=[pl.BlockSpec((1, tm, d), row),
                  pl.BlockSpec((1, d), lambda b, t: (0, 0)),
                  pl.BlockSpec((1, 1, d), per_b),
                  pl.BlockSpec((1, 1, d), per_b),
                  _const_spec((d, n))],
        out_specs=out_specs,
        out_shape=out_shape,
        scratch_shapes=scratch,
        compiler_params=_cparams("arbitrary", "arbitrary"),
        name="norm_mod_inproj",
    )(h, nw.reshape(1, d), sc, sh, w)


def _unit_lower_inverses(mats, ri, ci):
    n = mats[0].shape[0]
    eye = (ri == ci).astype(F32)
    pair = (ri // 2 == ci // 2) & (ri > ci)
    ts = [eye - jnp.where(pair, a, 0.0) for a in mats]
    s = 2
    while s < n:
        sel = (ri // (2 * s) == ci // (2 * s)) & (ri // s > ci // s)
        tmps = [_mm3(t, jnp.where(sel, a, 0.0)) for t, a in zip(ts, mats)]
        ts = [t - _mm3(tmp, t) for t, tmp in zip(ts, tmps)]
        s *= 2
    return ts


def _gdn_kernel(x_ref, z_ref, ba_ref, cw_ref, garow_ref, gbrow_ref, hnw_ref, o_ref,
                xbuf_ref, state_ref, *, heads, tile):
    dh = GDN_HEAD_DIM
    width = heads * dh
    ck = GDN_CHUNK
    nck = tile // ck
    t_id = pl.program_id(1)

    @pl.when(t_id == 0)
    def _():
        xbuf_ref[0:SUBLANES, :] = jnp.zeros((SUBLANES, 3 * width), F32)
        state_ref[...] = jnp.zeros(state_ref.shape, F32)

    x = x_ref[0]
    xbuf_ref[SUBLANES:SUBLANES + tile, :] = x
    conv = cw_ref[0:1, :] * xbuf_ref[pl.ds(SUBLANES - CONV_K + 1, tile), :]
    for k in range(1, CONV_K):
        conv = conv + cw_ref[k:k + 1, :] * xbuf_ref[pl.ds(SUBLANES - CONV_K + 1 + k, tile), :]
    xbuf_ref[0:SUBLANES, :] = x[tile - SUBLANES:tile, :]
    qkv = _silu(conv)

    ba = ba_ref[0]
    beta_all = _sigmoid(ba)
    g_all = -jnp.exp(garow_ref[...]) * _softplus(ba + gbrow_ref[...])
    ri_t = lax.broadcasted_iota(jnp.int32, (tile, tile), 0)
    ci_t = lax.broadcasted_iota(jnp.int32, (tile, tile), 1)
    cum_mat = ((ri_t // ck == ci_t // ck) & (ri_t >= ci_t)).astype(BF16)
    g_hi = g_all.astype(BF16)
    g_r1 = g_all - g_hi.astype(F32)
    g_mid = g_r1.astype(BF16)
    g_lo = (g_r1 - g_mid.astype(F32)).astype(BF16)
    gc_all = _dot(cum_mat, g_hi) + (_dot(cum_mat, g_mid) + _dot(cum_mat, g_lo))
    gc_all_t = gc_all.T

    ri = lax.broadcasted_iota(jnp.int32, (ck, ck), 0)
    ci = lax.broadcasted_iota(jnp.int32, (ck, ck), 1)
    hnw = hnw_ref[...]
    z_all = z_ref[0]

    prep = []
    for n in range(nck):
        sl = slice(n * ck, (n + 1) * ck)
        for h in range(heads):
            q = qkv[sl, h * dh:(h + 1) * dh]
            k = qkv[sl, width + h * dh:width + (h + 1) * dh]
            v = qkv[sl, 2 * width + h * dh:2 * width + (h + 1) * dh]
            q = q * lax.rsqrt(jnp.sum(q * q, axis=-1, keepdims=True) + NORM_EPS) * (dh ** -0.5)
            k = k * lax.rsqrt(jnp.sum(k * k, axis=-1, keepdims=True) + NORM_EPS)
            beta = beta_all[sl, h:h + 1]
            gcc = gc_all[sl, heads + h:heads + h + 1]
            gcr = gc_all_t[heads + h:heads + h + 1, sl]
            g_last = gcc[ck - 1:ck, :]
            decay = jnp.where(ri >= ci, jnp.exp(jnp.minimum(gcc - gcr, 0.0)), 0.0)
            k_beta = k * beta
            kb16, k16 = k_beta.astype(BF16), k.astype(BF16)
            egc = jnp.exp(gcc)
            prep.append(dict(
                a=jnp.where(ri > ci, _dot_nt(kb16, k16) * decay, 0.0),
                rhs=jnp.concatenate([v * beta, k_beta * egc], axis=-1),
                attn=(_dot_nt(q.astype(BF16), k16) * decay).astype(BF16),
                q_dec=(q * egc).astype(BF16),
                k_dec=(k * jnp.exp(g_last - gcc)).astype(BF16),
                s_decay=jnp.exp(g_last)))
    invs = _unit_lower_inverses([p["a"] for p in prep], ri, ci)
    sols = [_mm3(t, p["rhs"]) for t, p in zip(invs, prep)]

    states = [state_ref[h] for h in range(heads)]
    outs = [[] for _ in range(heads)]
    for n in range(nck):
        cur = [(prep[n * heads + h], sols[n * heads + h]) for h in range(heads)]
        s16 = [st.astype(BF16) for st in states]
        vn16 = [(sol[:, :dh] - _dot(sol[:, dh:].astype(BF16), s)).astype(BF16) for (_, sol), s in zip(cur, s16)]
        for h in range(heads):
            p = cur[h][0]
            outs[h].append(_dot(p["q_dec"], s16[h]) + _dot(p["attn"], vn16[h]))
        states = [st * cur[h][0]["s_decay"] + _dot_tn(cur[h][0]["k_dec"], vn16[h]) for h, st in enumerate(states)]
    for h in range(heads):
        state_ref[h] = states[h]
        o_h = jnp.concatenate(outs[h], axis=0) if nck > 1 else outs[h][0]
        o_h = (_rms(o_h) * hnw) * _silu(z_all[:, h * dh:(h + 1) * dh])
        o_ref[0, :, h * dh:(h + 1) * dh] = o_h.astype(o_ref.dtype)


def _gdn(qkv_pre, z, ba, conv_w, a_log, dt_bias, head_norm_w):
    bsz, seq, w3 = qkv_pre.shape
    width = w3 // 3
    heads = width // GDN_HEAD_DIM
    tile = min(GDN_TILE, seq)
    garow = jnp.zeros((1, LANES), F32).at[0, heads:2 * heads].set(a_log.astype(F32))
    gbrow = jnp.zeros((1, LANES), F32).at[0, heads:2 * heads].set(dt_bias.astype(F32))
    row = lambda b, t: (b, t, 0)
    fix2 = lambda b, t: (0, 0)
    return pl.pallas_call(
        functools.partial(_gdn_kernel, heads=heads, tile=tile),
        grid=(bsz, seq // tile),
        in_specs=[pl.BlockSpec((1, tile, w3), row),
                  pl.BlockSpec((1, tile, width), row),
                  pl.BlockSpec((1, tile, LANES), row),
                  pl.BlockSpec((CONV_K, w3), fix2),
                  pl.BlockSpec((1, LANES), fix2),
                  pl.BlockSpec((1, LANES), fix2),
                  pl.BlockSpec((1, GDN_HEAD_DIM), fix2)],
        out_specs=pl.BlockSpec((1, tile, width), row),
        out_shape=jax.ShapeDtypeStruct((bsz, seq, width), BF16),
        scratch_shapes=[pltpu.VMEM((SUBLANES + tile, w3), F32),
                        pltpu.VMEM((heads, GDN_HEAD_DIM, GDN_HEAD_DIM), F32)],
        compiler_params=_cparams("arbitrary", "arbitrary"),
        name="gated_deltanet",
    )(qkv_pre, z, ba, conv_w.astype(F32), garow, gbrow, head_norm_w.astype(F32).reshape(1, -1))


def _s5_prep_kernel(lr_ref, li_ref, ldt_ref, btr_ref, bti_ref, cr_ref, ci_ref,
                    kst_ref, bjr_ref, bji_ref, cor_ref, coi_ref, l16r_ref, l16i_ref):
    nt = S5_CHUNK
    lr, li = lr_ref[0], li_ref[0]
    dt = jnp.exp(ldt_ref[0])
    mag = jnp.exp(lr * dt)
    lb_re, lb_im = mag * jnp.cos(li * dt), mag * jnp.sin(li * dt)
    den = lr * lr + li * li
    num_re, num_im = lb_re - 1.0, lb_im
    f_re = (num_re * lr + num_im * li) / den
    f_im = (num_im * lr - num_re * li) / den
    btr, bti = btr_ref[0], bti_ref[0]
    bb_re = f_re * btr - f_im * bti
    bb_im = f_re * bti + f_im * btr
    cr, ci = cr_ref[0], ci_ref[0]
    p_re, p_im = [jnp.ones_like(lr)], [jnp.zeros_like(lr)]
    for _ in range(nt):
        p_re.append(p_re[-1] * lb_re - p_im[-1] * lb_im)
        p_im.append(p_re[-2] * lb_im + p_im[-1] * lb_re)
    cp_re = [cr * pr - ci * pi for pr, pi in zip(p_re, p_im)]
    cp_im = [cr * pi + ci * pr for pr, pi in zip(p_re, p_im)]
    kst_ref[0] = (_mm3_nt(jnp.concatenate(cp_re[:nt], axis=0), bb_re)
                  - _mm3_nt(jnp.concatenate(cp_im[:nt], axis=0), bb_im))
    bjr_ref[0] = jnp.concatenate([bb_re * p_re[nt - 1 - s] - bb_im * p_im[nt - 1 - s] for s in range(nt)], axis=0)
    bji_ref[0] = jnp.concatenate([bb_re * p_im[nt - 1 - s] + bb_im * p_re[nt - 1 - s] for s in range(nt)], axis=0)
    cor_ref[0] = jnp.concatenate(cp_re[1:nt + 1], axis=0)
    coi_ref[0] = jnp.concatenate(cp_im[1:nt + 1], axis=0)
    l16r_ref[0] = p_re[nt]
    l16i_ref[0] = p_im[nt]


def _s5_prep(lam_re, lam_im, log_dt, b_re, b_im, c_re, c_im):
    g, p = lam_re.shape
    hs = c_re.shape[1]
    nt = S5_CHUNK
    vec = pl.BlockSpec((1, 1, p), lambda i: (i, 0, 0))
    mat = pl.BlockSpec((1, hs, p), lambda i: (i, 0, 0))
    big = pl.BlockSpec((1, nt * hs, p), lambda i: (i, 0, 0))
    f32 = lambda a: a.astype(F32)
    return pl.pallas_call(
        _s5_prep_kernel,
        grid=(g,),
        in_specs=[vec, vec, pl.BlockSpec((1, 1, 1), lambda i: (i, 0, 0)), mat, mat, mat, mat],
        out_specs=[pl.BlockSpec((1, nt * hs, hs), lambda i: (i, 0, 0)), big, big, big, big, vec, vec],
        out_shape=[jax.ShapeDtypeStruct((g, nt * hs, hs), F32)]
                  + [jax.ShapeDtypeStruct((g, nt * hs, p), F32)] * 4
                  + [jax.ShapeDtypeStruct((g, 1, p), F32)] * 2,
        compiler_params=_cparams("arbitrary"),
        name="s5_discretise",
    )(f32(lam_re).reshape(g, 1, p), f32(lam_im).reshape(g, 1, p), f32(log_dt).reshape(g, 1, 1),
      f32(b_re).transpose(0, 2, 1), f32(b_im).transpose(0, 2, 1), f32(c_re), f32(c_im))


def _s5_kernel(u_ref, toep_ref, binj_ref, ct_ref, dt_ref, l16r_ref, l16i_ref, y_ref,
               sre_ref, sim_ref, xre_ref, xim_ref, *, streams):
    rows = u_ref.shape[1]
    steps = rows // streams
    u = u_ref[0]
    ub = u.astype(BF16)
    inj = _dot(ub, binj_ref[0])
    sre_ref[...] = inj[:, :LANES]
    sim_ref[...] = inj[:, LANES:]
    lr = jnp.broadcast_to(l16r_ref[0], (streams, LANES))
    li = jnp.broadcast_to(l16i_ref[0], (streams, LANES))

    def scan_step(i, carry):
        xr, xi = carry
        at = pl.ds(i, streams, stride=steps)
        xre_ref[at, :] = xr
        xim_ref[at, :] = xi
        sr = sre_ref[at, :]
        si = sim_ref[at, :]
        return lr * xr - li * xi + sr, lr * xi + li * xr + si

    zero = jnp.zeros((streams, LANES), F32)
    lax.fori_loop(0, steps, scan_step, (zero, zero), unroll=8)

    xin = jnp.concatenate([xre_ref[...], xim_ref[...]], axis=1).astype(BF16)
    y_ref[0] = _dot(ub, toep_ref[0]) + _dot(xin, ct_ref[0]) + dt_ref[0] * u


def _s5(ut, bsz, lam_re, lam_im, log_dt, b_re, b_im, c_re, c_im, d_skip):
    g, rows, _ = ut.shape
    hs, p, nt = S5_GROUP, S5_STATE, S5_CHUNK
    streams = bsz
    assert p * 2 == LANES and nt * hs == 2 * LANES
    steps = rows // streams
    kst, bjr, bji, cor, coi, l16r, l16i = _s5_prep(lam_re, lam_im, log_dt, b_re, b_im, c_re, c_im)

    per_block = LANES // hs
    row_w = nt * hs
    place = np.zeros((nt, nt, nt), np.float32)
    lane = np.zeros((per_block, row_w, row_w), np.float32)
    for s in range(nt):
        for t in range(s, nt):
            place[t - s, s, t] = 1.0
        for j in range(per_block):
            for hi in range(hs):
                lane[j, s * hs + hi, _s5_lane(j, s, hi)] = 1.0
    exact = functools.partial(jnp.einsum, precision=lax.Precision.HIGHEST)
    by_block = lambda a: a.reshape((g // per_block, per_block) + a.shape[1:])
    k4 = kst.reshape(g, nt, hs, hs)
    toep = exact("xst,gxoh->gshto", place, k4).reshape(g, row_w, row_w)
    toep = exact("jrd,ljrc->ljdc", lane, by_block(toep))
    toep = exact("ljrc,jcd->ljrd", toep, lane).reshape(g, row_w, row_w).astype(BF16)
    zpad = jnp.zeros((g, row_w, LANES - p), F32)
    binj = jnp.concatenate([bjr, zpad, bji, zpad], axis=-1)
    binj = exact("jrd,ljrc->ljdc", lane, by_block(binj)).reshape(g, row_w, 2 * LANES).astype(BF16)
    zrow = jnp.zeros((g, LANES - p, row_w), F32)
    ct = jnp.concatenate([cor.transpose(0, 2, 1), zrow, -coi.transpose(0, 2, 1), zrow], axis=1)
    ct = exact("ljrc,jcd->ljrd", by_block(ct), lane).reshape(g, 2 * LANES, row_w).astype(BF16)
    dtile = jnp.tile(d_skip.astype(F32).reshape(g, 1, hs), (1, 1, nt))
    lpad = jnp.zeros((g, 1, LANES - p), F32)
    l16r = jnp.concatenate([l16r, lpad], axis=-1)
    l16i = jnp.concatenate([l16i, lpad], axis=-1)

    per_g3 = lambda i: (i, 0, 0)
    return pl.pallas_call(
        functools.partial(_s5_kernel, streams=streams),
        grid=(g,),
        in_specs=[pl.BlockSpec((1, rows, nt * hs), per_g3),
                  pl.BlockSpec((1, nt * hs, nt * hs), per_g3),
                  pl.BlockSpec((1, nt * hs, 2 * LANES), per_g3),
                  pl.BlockSpec((1, 2 * LANES, nt * hs), per_g3),
                  pl.BlockSpec((1, 1, nt * hs), per_g3),
                  pl.BlockSpec((1, 1, LANES), per_g3),
                  pl.BlockSpec((1, 1, LANES), per_g3)],
        out_specs=pl.BlockSpec((1, rows, nt * hs), per_g3),
        out_shape=jax.ShapeDtypeStruct((g, rows, nt * hs), F32),
        scratch_shapes=[pltpu.VMEM((rows, LANES), F32)] * 4,
        compiler_params=_cparams("arbitrary"),
        name="s5_chunked_scan",
    )(ut, toep, binj, ct, dtile, l16r, l16i)


def _sb_kernel(q_ref, k_ref, v_ref, o_ref, acc_ref, run_ref, *, qb, kb, nsub):
    qi = pl.program_id(2)
    per_key = kb // qb
    pairs = nsub // per_key
    ri = lax.broadcasted_iota(jnp.int32, (kb, kb), 0)
    ci = lax.broadcasted_iota(jnp.int32, (kb, kb), 1)
    after = (ri > ci).astype(BF16)
    rq = lax.broadcasted_iota(jnp.int32, (qb, kb), 0)
    cq = lax.broadcasted_iota(jnp.int32, (qb, kb), 1)
    diag_valid = [cq < rq + r * qb for r in range(per_key)]
    base = pairs * qi

    def pre(sub, kblk, valid):
        s = _dot_nt(q_ref[0, 0, sub * qb:(sub + 1) * qb, :], kblk)
        lse = jnp.log2(1.0 + jnp.exp2(-jnp.abs(s)))
        log_p = jnp.minimum(s, 0.0) - lse
        neg_log_1m = jnp.maximum(s, 0.0) + lse
        if valid is not None:
            neg_log_1m = jnp.where(valid, neg_log_1m, 0.0)
        hi, lo = _split2(neg_log_1m)
        later = _dot(hi, after) + _dot(lo, after)
        return log_p - later, jnp.sum(neg_log_1m, axis=1, keepdims=True)

    def post(sub, part, tot, vblk, valid, exists):
        rows = slice(sub * qb, (sub + 1) * qb)
        run = run_ref[rows, :]
        if exists is not None:
            run = run + jnp.where(exists, 0.0, SB_NO_KEYS)
            tot = jnp.where(exists, tot, 0.0)
        wts = jnp.exp2(part - run)
        if valid is not None:
            wts = jnp.where(valid, wts, 0.0)
        acc_ref[rows, :] += _dot(wts.astype(BF16), vblk)
        run_ref[rows, :] = run + tot

    def band_pre(e, diagonal):
        items = []
        for m in range(pairs):
            idx = base + m - e
            exists = None if diagonal else idx >= 0
            k0 = pl.multiple_of(jnp.maximum(idx, 0) * kb, kb)
            kblk, vblk = k_ref[0, 0, pl.ds(k0, kb), :], v_ref[0, 0, pl.ds(k0, kb), :]
            for r in range(per_key):
                sub = m * per_key + r
                valid = diag_valid[r] if diagonal else None
                items.append((sub, vblk, valid, exists, pre(sub, kblk, valid)))
        return items

    def band_post(items):
        for sub, vblk, valid, exists, (part, tot) in items:
            post(sub, part, tot, vblk, valid, exists)

    acc_ref[...] = jnp.zeros(acc_ref.shape, F32)
    run_ref[...] = jnp.zeros(run_ref.shape, F32)
    first = [band_pre(e, e == 0) for e in range(SB_STATIC_BANDS)]
    for items in first:
        band_post(items)

    last = base + pairs - 1

    def keep_going(state):
        e, run_min = state
        return jnp.logical_and(e <= last, run_min < SB_UNDERFLOW_LOG2)

    def body(state):
        e, _ = state
        band_post(band_pre(e, False))
        return e + 1, jnp.min(run_ref[...])

    lax.while_loop(keep_going, body, (jnp.int32(SB_STATIC_BANDS), jnp.min(run_ref[...])))
    o_ref[0, 0] = acc_ref[...].astype(o_ref.dtype)


def _sb_attention(q, kv):
    bsz, heads, seq, d = q.shape
    kb = min(SB_KEY_BLOCK, seq)
    qb = min(SB_QUERY_BLOCK, kb)
    tq = min(SB_Q_TILE, seq)
    nsub = tq // qb
    return pl.pallas_call(
        functools.partial(_sb_kernel, qb=qb, kb=kb, nsub=nsub),
        grid=(bsz, heads, seq // tq),
        in_specs=[pl.BlockSpec((1, 1, tq, d), lambda b, h, i: (b, h, i, 0)),
                  pl.BlockSpec((1, 1, seq, d), lambda b, h, i: (b, h, 0, 0)),
                  pl.BlockSpec((1, 1, seq, d), lambda b, h, i: (b, heads + h, 0, 0))],
        out_specs=pl.BlockSpec((1, 1, tq, d), lambda b, h, i: (b, h, i, 0)),
        out_shape=jax.ShapeDtypeStruct((bsz, heads, seq, d), BF16),
        scratch_shapes=[pltpu.VMEM((tq, d), F32), pltpu.VMEM((tq, 1), F32)],
        compiler_params=_cparams("arbitrary", "arbitrary", "arbitrary"),
        name="stick_breaking_attention",
    )(q, kv, kv)


def _gelu_tanh(x):
    return 0.5 * x * (1.0 + jnp.tanh(0.7978845608028654 * (x + 0.044715 * (x * x * x))))


def _mix_ffn_kernel(*refs, even, final, f_chunks):
    if even:
        (h_ref, o_ref, y_ref, gw_ref, gb_ref, wtop_ref, wbot_ref, gtm_ref, nw_ref, sc_ref, sh_ref, gtf_ref,
         win_ref, wout_ref, fw_ref, out_ref, ybuf_ref) = refs
        y = _gelu_tanh(_from_chunk_major(y_ref, ybuf_ref))
        y = y * _sigmoid(_dot(y.astype(BF16), gw_ref[...]) + gb_ref[...])
        mix = _dot(o_ref[0], wtop_ref[...]) + _dot(y.astype(BF16), wbot_ref[...])
    else:
        (h_ref, o_ref, wtop_ref, gtm_ref, nw_ref, sc_ref, sh_ref, gtf_ref,
         win_ref, wout_ref, fw_ref, out_ref) = refs
        o = jnp.concatenate([o_ref[0, hh] for hh in range(o_ref.shape[1])], axis=1)
        mix = _dot(o, wtop_ref[...])
    h1 = h_ref[0] + gtm_ref[0] * mix
    hn = _norm_mod(h1, nw_ref[...], sc_ref[0], sh_ref[0]).astype(BF16)
    hidden = wout_ref.shape[0]
    fc = hidden // f_chunks
    acc = None
    for c in range(f_chunks):
        gate = _dot(hn, win_ref[:, c * fc:(c + 1) * fc])
        up = _dot(hn, win_ref[:, hidden + c * fc:hidden + (c + 1) * fc])
        part = _dot((_silu(gate) * up).astype(BF16), wout_ref[c * fc:(c + 1) * fc, :])
        acc = part if acc is None else acc + part
    h2 = h1 + gtf_ref[0] * acc
    if final:
        h2 = _rms(h2) * fw_ref[...]
    out_ref[0] = h2


def _mix_ffn(h, acts, mix_w, glu, gt_m, nw, sc, sh, gt_f, w_in, w_out, final_w, *, even, final):
    bsz, seq, d = h.shape
    tm = min(ROW_TILE, seq)
    hidden = w_out.shape[0]
    row = lambda b, t: (b, t, 0)
    per_b = lambda b, t: (b, 0, 0)
    vec_d = pl.BlockSpec((1, d), lambda b, t: (0, 0))
    mod_spec = pl.BlockSpec((1, 1, d), per_b)
    args, specs, scratch = [h], [pl.BlockSpec((1, tm, d), row)], []
    if even:
        o, y = acts
        half = o.shape[-1]
        nt = seq // tm
        glu_w, glu_b = glu
        args += [o, y, glu_w.astype(BF16), glu_b.astype(F32).reshape(1, -1),
                 mix_w[:half].astype(BF16), mix_w[half:].astype(BF16)]
        specs += [pl.BlockSpec((1, tm, half), row),
                  pl.BlockSpec((y.shape[0], tm // S5_CHUNK, y.shape[2]), lambda b, t: (0, b * nt + t, 0)),
                  _const_spec(glu_w.shape), pl.BlockSpec((1, glu_b.shape[-1]), lambda b, t: (0, 0)),
                  _const_spec((half, d)), _const_spec((d - half, d))]
        scratch = [pltpu.VMEM(((d - half) // LANES, tm, LANES), F32)]
    else:
        (o,) = acts
        args += [o, mix_w.astype(BF16)]
        specs += [pl.BlockSpec((1, o.shape[1], tm, o.shape[3]), lambda b, t: (b, 0, t, 0)), _const_spec((d, d))]
    args += [gt_m, nw.astype(F32).reshape(1, d), sc, sh, gt_f, w_in.astype(BF16), w_out.astype(BF16),
             final_w.astype(F32).reshape(1, d)]
    specs += [mod_spec, vec_d, mod_spec, mod_spec, mod_spec, _const_spec(w_in.shape), _const_spec(w_out.shape), vec_d]
    return pl.pallas_call(
        functools.partial(_mix_ffn_kernel, even=even, final=final, f_chunks=2),
        grid=(bsz, seq // tm),
        in_specs=specs,
        out_specs=pl.BlockSpec((1, tm, d), row),
        out_shape=jax.ShapeDtypeStruct((bsz, seq, d), F32),
        scratch_shapes=scratch,
        compiler_params=_cparams("arbitrary", "arbitrary"),
        name="outproj_swiglu_ffn",
    )(*args)


def kernel(x, c, ada_w, ada_b, norm_mix_w, norm_ffn_w, ffn_w_in, ffn_w_out, hy_w_in, hy_conv_w, hy_a_log, hy_dt_bias, hy_head_norm_w, s5_lam_re, s5_lam_im, s5_log_dt, s5_b_re, s5_b_im, s5_c_re, s5_c_im, s5_d, s5_glu_w, s5_glu_b, hy_w_out, sb_w_in, sb_w_out, final_norm_w):
    bsz, seq, d = x.shape
    depth = ada_w.shape[0]
    gdn_width = d // 2
    gdn_heads = gdn_width // GDN_HEAD_DIM
    s5_width = d - gdn_width

    mod = _modulation(c, ada_w, ada_b)
    h = x
    for i in range(depth):
        sh_m, sc_m, gt_m, sh_f, sc_f, gt_f = (mod[i, :bsz, k * d:(k + 1) * d].reshape(bsz, 1, d) for k in range(6))
        j = i // 2
        last = i == depth - 1
        if i % 2 == 0:
            w = hy_w_in[j]
            w4 = 4 * gdn_width
            gate_cols = jnp.pad(w[:, w4:w4 + 2 * gdn_heads], ((0, 0), (0, LANES - 2 * gdn_heads)))
            w_cat = jnp.concatenate([w[:, :w4], gate_cols, w[:, w4 + 2 * gdn_heads:]], axis=1).astype(BF16)
            qkv_pre, zg, ba, ut = _inproj(h, norm_mix_w[i], sc_m, sh_m, w_cat,
                                          (3 * gdn_width, gdn_width, LANES, s5_width), (F32, F32, F32, F32),
                                          chunk_major_last=True)
            o = _gdn(qkv_pre, zg, ba, hy_conv_w[j], hy_a_log[j], hy_dt_bias[j], hy_head_norm_w[j])
            y = _s5(ut, bsz, s5_lam_re[j], s5_lam_im[j], s5_log_dt[j], s5_b_re[j], s5_b_im[j],
                    s5_c_re[j], s5_c_im[j], s5_d[j])
            h = _mix_ffn(h, (o, y), hy_w_out[j], (s5_glu_w[j], s5_glu_b[j]), gt_m, norm_ffn_w[i], sc_f, sh_f, gt_f,
                         ffn_w_in[i], ffn_w_out[i], final_norm_w, even=True, final=last)
        else:
            q, kv = _inproj(h, norm_mix_w[i], sc_m, sh_m, sb_w_in[j].astype(BF16), (d, 2 * d), (BF16, BF16),
                            scales=(SB_HEAD_DIM ** -0.5 * LOG2_E, 1.0), head_dim=SB_HEAD_DIM)
            o = _sb_attention(q, kv)
            h = _mix_ffn(h, (o,), sb_w_out[j], None, gt_m, norm_ffn_w[i], sc_f, sh_f, gt_f,
                         ffn_w_in[i], ffn_w_out[i], final_norm_w, even=False, final=last)
    return h
```

```python
import functools

import jax
import jax.numpy as jnp
import numpy as np
from jax import lax
from jax.experimental import pallas as pl
from jax.experimental.pallas import tpu as pltpu

F32 = jnp.float32
BF16 = jnp.bfloat16
NORM_EPS = 1e-6

GDN_HEAD_DIM = 128
CONV_K = 4
GDN_CHUNK = 64
S5_GROUP = 16
S5_STATE = 64
SB_HEAD_DIM = 128

LANES = 128
SUBLANES = 8
VMEM_LIMIT_BYTES = 56 * 1024 * 1024

ROW_TILE = 512
GDN_TILE = 256
S5_CHUNK = 16
SB_KEY_BLOCK = 256
SB_QUERY_BLOCK = 256
SB_Q_TILE = 2048
SB_STATIC_BANDS = 2
SB_NO_KEYS = 1.0e6
LOG2_E = 1.4426950408889634
SB_UNDERFLOW_LOG2 = 160.0
MOD_COL_TILE = 1536


def _cparams(*sem):
    return pltpu.CompilerParams(dimension_semantics=sem, vmem_limit_bytes=VMEM_LIMIT_BYTES)


def _const_spec(shape):
    nd = len(shape)
    return pl.BlockSpec(shape, lambda *_: (0,) * nd, pipeline_mode=pl.Buffered(1))


def _dot(a, b):
    return jnp.dot(a, b, preferred_element_type=F32)


def _dot_nt(a, b):
    return lax.dot_general(a, b, (((1,), (1,)), ((), ())), preferred_element_type=F32)


def _dot_tn(a, b):
    return lax.dot_general(a, b, (((0,), (0,)), ((), ())), preferred_element_type=F32)


def _split2(x):
    hi = x.astype(BF16)
    lo = (x - hi.astype(F32)).astype(BF16)
    return hi, lo


def _mm3(a, b):
    ah, al = _split2(a)
    bh, bl = _split2(b)
    return _dot(ah, bh) + (_dot(ah, bl) + _dot(al, bh))


def _mm3_nt(a, b):
    ah, al = _split2(a)
    bh, bl = _split2(b)
    return _dot_nt(ah, bh) + (_dot_nt(ah, bl) + _dot_nt(al, bh))


def _sigmoid(x):
    return 1.0 / (1.0 + jnp.exp(-x))


def _silu(x):
    return x * _sigmoid(x)


def _softplus(x):
    return jnp.maximum(x, 0.0) + jnp.log(1.0 + jnp.exp(-jnp.abs(x)))


def _rms(x):
    return x * lax.rsqrt(jnp.mean(x * x, axis=-1, keepdims=True) + NORM_EPS)


def _norm_mod(x, nw, sc, sh):
    return (_rms(x) * nw) * (1.0 + sc) + sh


def _mod_kernel(c_ref, w_ref, b_ref, o_ref):
    ca = _silu(c_ref[...])
    o_ref[0] = _mm3(ca, w_ref[0]) + b_ref[0]


def _modulation(c, ada_w, ada_b):
    depth, d, n = ada_w.shape
    bsz = c.shape[0]
    rows = -(-bsz // SUBLANES) * SUBLANES
    c_pad = jnp.pad(c, ((0, rows - bsz), (0, 0)))
    tn = MOD_COL_TILE
    return pl.pallas_call(
        _mod_kernel,
        grid=(depth, n // tn),
        in_specs=[pl.BlockSpec((rows, d), lambda i, j: (0, 0)),
                  pl.BlockSpec((1, d, tn), lambda i, j: (i, 0, j)),
                  pl.BlockSpec((1, 1, tn), lambda i, j: (i, 0, j))],
        out_specs=pl.BlockSpec((1, rows, tn), lambda i, j: (i, 0, j)),
        out_shape=jax.ShapeDtypeStruct((depth, rows, n), F32),
        compiler_params=_cparams("arbitrary", "arbitrary"),
        name="adaln_modulation",
    )(c_pad, ada_w, ada_b.reshape(depth, 1, n))


def _s5_lane(group, s, hi=0):
    per_block = LANES // S5_GROUP
    return (s // per_block) * LANES + ((s + group) % per_block) * S5_GROUP + hi


def _to_chunk_major(x, z_ref, buf_ref):
    tokens, width = x.shape
    chunks = tokens // S5_CHUNK
    per_block = LANES // S5_GROUP
    for lb in range(width // LANES):
        buf_ref[lb] = x[:, lb * LANES:(lb + 1) * LANES]
    for s in range(S5_CHUNK):
        shift = (s % per_block) * S5_GROUP
        for lb in range(width // LANES):
            rows = buf_ref[lb, pl.ds(s, chunks, stride=S5_CHUNK), :]
            rows = pltpu.roll(rows, shift, axis=1) if shift else rows
            for j in range(per_block):
                src = ((s + j) % per_block) * S5_GROUP
                dst = _s5_lane(lb * per_block + j, s)
                z_ref[lb * per_block + j, :, dst:dst + S5_GROUP] = rows[:, src:src + S5_GROUP]


def _from_chunk_major(z_ref, buf_ref):
    groups, chunks, _ = z_ref.shape
    per_block = LANES // S5_GROUP
    nlb = groups // per_block
    lane_block = lax.broadcasted_iota(jnp.int32, (chunks, LANES), 1) // S5_GROUP
    for s in range(S5_CHUNK):
        half = s // per_block
        shift = (s % per_block) * S5_GROUP
        for lb in range(nlb):
            rows = None
            for j in range(per_block):
                piece = z_ref[lb * per_block + j, :, half * LANES:(half + 1) * LANES]
                rows = piece if rows is None else jnp.where(lane_block == (s + j) % per_block, piece, rows)
            rows = pltpu.roll(rows, LANES - shift, axis=1) if shift else rows
            buf_ref[lb, pl.ds(s, chunks, stride=S5_CHUNK), :] = rows
    return jnp.concatenate([buf_ref[lb] for lb in range(nlb)], axis=1)


def _inproj_kernel(h_ref, nw_ref, sc_ref, sh_ref, w_ref, *refs, widths, scales, chunk_major_last, head_dim):
    out_refs = refs[:len(widths)]
    hn = _norm_mod(h_ref[0], nw_ref[...], sc_ref[0], sh_ref[0]).astype(BF16)
    off = 0
    for idx, (o_ref, wd, s) in enumerate(zip(out_refs, widths, scales)):
        acc = _dot(hn, w_ref[:, off:off + wd])
        if s != 1.0:
            acc = acc * s
        if chunk_major_last and idx == len(widths) - 1:
            _to_chunk_major(acc, o_ref, refs[len(widths)])
        elif head_dim:
            for hh in range(wd // head_dim):
                o_ref[0, hh] = acc[:, hh * head_dim:(hh + 1) * head_dim].astype(o_ref.dtype)
        else:
            o_ref[0] = acc.astype(o_ref.dtype)
        off += wd


def _inproj(h, nw, sc, sh, w, widths, dtypes, scales=None, chunk_major_last=False, head_dim=0):
    scales = scales or (1.0,) * len(widths)
    bsz, seq, d = h.shape
    tm = min(ROW_TILE, seq)
    nt = seq // tm
    n = w.shape[1]
    row = lambda b, t: (b, t, 0)
    per_b = lambda b, t: (b, 0, 0)
    if head_dim:
        out_specs = [pl.BlockSpec((1, wd // head_dim, tm, head_dim), lambda b, t: (b, 0, t, 0)) for wd in widths]
        out_shape = [jax.ShapeDtypeStruct((bsz, wd // head_dim, seq, head_dim), dt) for wd, dt in zip(widths, dtypes)]
    else:
        out_specs = [pl.BlockSpec((1, tm, wd), row) for wd in widths]
        out_shape = [jax.ShapeDtypeStruct((bsz, seq, wd), dt) for wd, dt in zip(widths, dtypes)]
    scratch = []
    if chunk_major_last:
        groups = widths[-1] // S5_GROUP
        row_w = S5_CHUNK * S5_GROUP
        out_specs[-1] = pl.BlockSpec((groups, tm // S5_CHUNK, row_w), lambda b, t: (0, b * nt + t, 0))
        out_shape[-1] = jax.ShapeDtypeStruct((groups, bsz * seq // S5_CHUNK, row_w), dtypes[-1])
        scratch = [pltpu.VMEM((widths[-1] // LANES, tm, LANES), F32)]
    return pl.pallas_call(
        functools.partial(_inproj_kernel, widths=widths, scales=scales, chunk_major_last=chunk_major_last,
                          head_dim=head_dim),
        grid=(bsz, nt),
        in_specs=[pl.BlockSpec((1, tm, d), row),
                  pl.BlockSpec((1, d), lambda b, t: (0, 0)),
                  pl.BlockSpec((1, 1, d), per_b),
                  pl.BlockSpec((1, 1, d), per_b),
                  _const_spec((d, n))],
        out_specs=out_specs,
        out_shape=out_shape,
        scratch_shapes=scratch,
        compiler_params=_cparams("arbitrary", "arbitrary"),
        name="norm_mod_inproj",
    )(h, nw.reshape(1, d), sc, sh, w)


def _unit_lower_inverses(mats, ri, ci):
    n = mats[0].shape[0]
    eye = (ri == ci).astype(F32)
    pair = (ri // 2 == ci // 2) & (ri > ci)
    ts = [eye - jnp.where(pair, a, 0.0) for a in mats]
    s = 2
    while s < n:
        sel = (ri // (2 * s) == ci // (2 * s)) & (ri // s > ci // s)
        tmps = [_mm3(t, jnp.where(sel, a, 0.0)) for t, a in zip(ts, mats)]
        ts = [t - _mm3(tmp, t) for t, tmp in zip(ts, tmps)]
        s *= 2
    return ts


def _gdn_kernel(x_ref, z_ref, ba_ref, cw_ref, garow_ref, gbrow_ref, hnw_ref, o_ref,
                xbuf_ref, state_ref, *, heads, tile):
    dh = GDN_HEAD_DIM
    width = heads * dh
    ck = GDN_CHUNK
    nck = tile // ck
    t_id = pl.program_id(1)

    @pl.when(t_id == 0)
    def _():
        xbuf_ref[0:SUBLANES, :] = jnp.zeros((SUBLANES, 3 * width), F32)
        state_ref[...] = jnp.zeros(state_ref.shape, F32)

    x = x_ref[0]
    xbuf_ref[SUBLANES:SUBLANES + tile, :] = x
    conv = cw_ref[0:1, :] * xbuf_ref[pl.ds(SUBLANES - CONV_K + 1, tile), :]
    for k in range(1, CONV_K):
        conv = conv + cw_ref[k:k + 1, :] * xbuf_ref[pl.ds(SUBLANES - CONV_K + 1 + k, tile), :]
    xbuf_ref[0:SUBLANES, :] = x[tile - SUBLANES:tile, :]
    qkv = _silu(conv)

    ba = ba_ref[0]
    beta_all = _sigmoid(ba)
    g_all = -jnp.exp(garow_ref[...]) * _softplus(ba + gbrow_ref[...])
    ri_t = lax.broadcasted_iota(jnp.int32, (tile, tile), 0)
    ci_t = lax.broadcasted_iota(jnp.int32, (tile, tile), 1)
    cum_mat = ((ri_t // ck == ci_t // ck) & (ri_t >= ci_t)).astype(BF16)
    g_hi = g_all.astype(BF16)
    g_r1 = g_all - g_hi.astype(F32)
    g_mid = g_r1.astype(BF16)
    g_lo = (g_r1 - g_mid.astype(F32)).astype(BF16)
    gc_all = _dot(cum_mat, g_hi) + (_dot(cum_mat, g_mid) + _dot(cum_mat, g_lo))
    gc_all_t = gc_all.T

    ri = lax.broadcasted_iota(jnp.int32, (ck, ck), 0)
    ci = lax.broadcasted_iota(jnp.int32, (ck, ck), 1)
    hnw = hnw_ref[...]
    z_all = z_ref[0]

    prep = []
    for n in range(nck):
        sl = slice(n * ck, (n + 1) * ck)
        for h in range(heads):
            q = qkv[sl, h * dh:(h + 1) * dh]
            k = qkv[sl, width + h * dh:width + (h + 1) * dh]
            v = qkv[sl, 2 * width + h * dh:2 * width + (h + 1) * dh]
            q = q * lax.rsqrt(jnp.sum(q * q, axis=-1, keepdims=True) + NORM_EPS) * (dh ** -0.5)
            k = k * lax.rsqrt(jnp.sum(k * k, axis=-1, keepdims=True) + NORM_EPS)
            beta = beta_all[sl, h:h + 1]
            gcc = gc_all[sl, heads + h:heads + h + 1]
            gcr = gc_all_t[heads + h:heads + h + 1, sl]
            g_last = gcc[ck - 1:ck, :]
            decay = jnp.where(ri >= ci, jnp.exp(jnp.minimum(gcc - gcr, 0.0)), 0.0)
            k_beta = k * beta
            kb16, k16 = k_beta.astype(BF16), k.astype(BF16)
            egc = jnp.exp(gcc)
            prep.append(dict(
                a=jnp.where(ri > ci, _dot_nt(kb16, k16) * decay, 0.0),
                rhs=jnp.concatenate([v * beta, k_beta * egc], axis=-1),
                attn=(_dot_nt(q.astype(BF16), k16) * decay).astype(BF16),
                q_dec=(q * egc).astype(BF16),
                k_dec=(k * jnp.exp(g_last - gcc)).astype(BF16),
                s_decay=jnp.exp(g_last)))
    invs = _unit_lower_inverses([p["a"] for p in prep], ri, ci)
    sols = [_mm3(t, p["rhs"]) for t, p in zip(invs, prep)]

    states = [state_ref[h] for h in range(heads)]
    outs = [[] for _ in range(heads)]
    for n in range(nck):
        cur = [(prep[n * heads + h], sols[n * heads + h]) for h in range(heads)]
        s16 = [st.astype(BF16) for st in states]
        vn16 = [(sol[:, :dh] - _dot(sol[:, dh:].astype(BF16), s)).astype(BF16) for (_, sol), s in zip(cur, s16)]
        for h in range(heads):
            p = cur[h][0]
            outs[h].append(_dot(p["q_dec"], s16[h]) + _dot(p["attn"], vn16[h]))
        states = [st * cur[h][0]["s_decay"] + _dot_tn(cur[h][0]["k_dec"], vn16[h]) for h, st in enumerate(states)]
    for h in range(heads):
        state_ref[h] = states[h]
        o_h = jnp.concatenate(outs[h], axis=0) if nck > 1 else outs[h][0]
        o_h = (_rms(o_h) * hnw) * _silu(z_all[:, h * dh:(h + 1) * dh])
        o_ref[0, :, h * dh:(h + 1) * dh] = o_h.astype(o_ref.dtype)


def _gdn(qkv_pre, z, ba, conv_w, a_log, dt_bias, head_norm_w):
    bsz, seq, w3 = qkv_pre.shape
    width = w3 // 3
    heads = width // GDN_HEAD_DIM
    tile = min(GDN_TILE, seq)
    garow = jnp.zeros((1, LANES), F32).at[0, heads:2 * heads].set(a_log.astype(F32))
    gbrow = jnp.zeros((1, LANES), F32).at[0, heads:2 * heads].set(dt_bias.astype(F32))
    row = lambda b, t: (b, t, 0)
    fix2 = lambda b, t: (0, 0)
    return pl.pallas_call(
        functools.partial(_gdn_kernel, heads=heads, tile=tile),
        grid=(bsz, seq // tile),
        in_specs=[pl.BlockSpec((1, tile, w3), row),
                  pl.BlockSpec((1, tile, width), row),
                  pl.BlockSpec((1, tile, LANES), row),
                  pl.BlockSpec((CONV_K, w3), fix2),
                  pl.BlockSpec((1, LANES), fix2),
                  pl.BlockSpec((1, LANES), fix2),
                  pl.BlockSpec((1, GDN_HEAD_DIM), fix2)],
        out_specs=pl.BlockSpec((1, tile, width), row),
        out_shape=jax.ShapeDtypeStruct((bsz, seq, width), BF16),
        scratch_shapes=[pltpu.VMEM((SUBLANES + tile, w3), F32),
                        pltpu.VMEM((heads, GDN_HEAD_DIM, GDN_HEAD_DIM), F32)],
        compiler_params=_cparams("arbitrary", "arbitrary"),
        name="gated_deltanet",
    )(qkv_pre, z, ba, conv_w.astype(F32), garow, gbrow, head_norm_w.astype(F32).reshape(1, -1))


def _s5_prep_kernel(lr_ref, li_ref, ldt_ref, btr_ref, bti_ref, cr_ref, ci_ref,
                    kst_ref, bjr_ref, bji_ref, cor_ref, coi_ref, l16r_ref, l16i_ref):
    nt = S5_CHUNK
    lr, li = lr_ref[0], li_ref[0]
    dt = jnp.exp(ldt_ref[0])
    mag = jnp.exp(lr * dt)
    lb_re, lb_im = mag * jnp.cos(li * dt), mag * jnp.sin(li * dt)
    den = lr * lr + li * li
    num_re, num_im = lb_re - 1.0, lb_im
    f_re = (num_re * lr + num_im * li) / den
    f_im = (num_im * lr - num_re * li) / den
    btr, bti = btr_ref[0], bti_ref[0]
    bb_re = f_re * btr - f_im * bti
    bb_im = f_re * bti + f_im * btr
    cr, ci = cr_ref[0], ci_ref[0]
    p_re, p_im = [jnp.ones_like(lr)], [jnp.zeros_like(lr)]
    for _ in range(nt):
        p_re.append(p_re[-1] * lb_re - p_im[-1] * lb_im)
        p_im.append(p_re[-2] * lb_im + p_im[-1] * lb_re)
    cp_re = [cr * pr - ci * pi for pr, pi in zip(p_re, p_im)]
    cp_im = [cr * pi + ci * pr for pr, pi in zip(p_re, p_im)]
    kst_ref[0] = (_mm3_nt(jnp.concatenate(cp_re[:nt], axis=0), bb_re)
                  - _mm3_nt(jnp.concatenate(cp_im[:nt], axis=0), bb_im))
    bjr_ref[0] = jnp.concatenate([bb_re * p_re[nt - 1 - s] - bb_im * p_im[nt - 1 - s] for s in range(nt)], axis=0)
    bji_ref[0] = jnp.concatenate([bb_re * p_im[nt - 1 - s] + bb_im * p_re[nt - 1 - s] for s in range(nt)], axis=0)
    cor_ref[0] = jnp.concatenate(cp_re[1:nt + 1], axis=0)
    coi_ref[0] = jnp.concatenate(cp_im[1:nt + 1], axis=0)
    l16r_ref[0] = p_re[nt]
    l16i_ref[0] = p_im[nt]


def _s5_prep(lam_re, lam_im, log_dt, b_re, b_im, c_re, c_im):
    g, p = lam_re.shape
    hs = c_re.shape[1]
    nt = S5_CHUNK
    vec = pl.BlockSpec((1, 1, p), lambda i: (i, 0, 0))
    mat = pl.BlockSpec((1, hs, p), lambda i: (i, 0, 0))
    big = pl.BlockSpec((1, nt * hs, p), lambda i: (i, 0, 0))
    f32 = lambda a: a.astype(F32)
    return pl.pallas_call(
        _s5_prep_kernel,
        grid=(g,),
        in_specs=[vec, vec, pl.BlockSpec((1, 1, 1), lambda i: (i, 0, 0)), mat, mat, mat, mat],
        out_specs=[pl.BlockSpec((1, nt * hs, hs), lambda i: (i, 0, 0)), big, big, big, big, vec, vec],
        out_shape=[jax.ShapeDtypeStruct((g, nt * hs, hs), F32)]
                  + [jax.ShapeDtypeStruct((g, nt * hs, p), F32)] * 4
                  + [jax.ShapeDtypeStruct((g, 1, p), F32)] * 2,
        compiler_params=_cparams("arbitrary"),
        name="s5_discretise",
    )(f32(lam_re).reshape(g, 1, p), f32(lam_im).reshape(g, 1, p), f32(log_dt).reshape(g, 1, 1),
      f32(b_re).transpose(0, 2, 1), f32(b_im).transpose(0, 2, 1), f32(c_re), f32(c_im))


def _s5_kernel(u_ref, toep_ref, binj_ref, ct_ref, dt_ref, l16r_ref, l16i_ref, y_ref,
               sre_ref, sim_ref, xre_ref, xim_ref, *, streams):
    rows = u_ref.shape[1]
    steps = rows // streams
    u = u_ref[0]
    ub = u.astype(BF16)
    inj = _dot(ub, binj_ref[0])
    sre_ref[...] = inj[:, :LANES]
    sim_ref[...] = inj[:, LANES:]
    lr = jnp.broadcast_to(l16r_ref[0], (streams, LANES))
    li = jnp.broadcast_to(l16i_ref[0], (streams, LANES))

    def scan_step(i, carry):
        xr, xi = carry
        at = pl.ds(i, streams, stride=steps)
        xre_ref[at, :] = xr
        xim_ref[at, :] = xi
        sr = sre_ref[at, :]
        si = sim_ref[at, :]
        return lr * xr - li * xi + sr, lr * xi + li * xr + si

    zero = jnp.zeros((streams, LANES), F32)
    lax.fori_loop(0, steps, scan_step, (zero, zero), unroll=8)

    xin = jnp.concatenate([xre_ref[...], xim_ref[...]], axis=1).astype(BF16)
    y_ref[0] = _dot(ub, toep_ref[0]) + _dot(xin, ct_ref[0]) + dt_ref[0] * u


def _s5(ut, bsz, lam_re, lam_im, log_dt, b_re, b_im, c_re, c_im, d_skip):
    g, rows, _ = ut.shape
    hs, p, nt = S5_GROUP, S5_STATE, S5_CHUNK
    streams = bsz
    assert p * 2 == LANES and nt * hs == 2 * LANES
    steps = rows // streams
    kst, bjr, bji, cor, coi, l16r, l16i = _s5_prep(lam_re, lam_im, log_dt, b_re, b_im, c_re, c_im)

    per_block = LANES // hs
    row_w = nt * hs
    place = np.zeros((nt, nt, nt), np.float32)
    lane = np.zeros((per_block, row_w, row_w), np.float32)
    for s in range(nt):
        for t in range(s, nt):
            place[t - s, s, t] = 1.0
        for j in range(per_block):
            for hi in range(hs):
                lane[j, s * hs + hi, _s5_lane(j, s, hi)] = 1.0
    exact = functools.partial(jnp.einsum, precision=lax.Precision.HIGHEST)
    by_block = lambda a: a.reshape((g // per_block, per_block) + a.shape[1:])
    k4 = kst.reshape(g, nt, hs, hs)
    toep = exact("xst,gxoh->gshto", place, k4).reshape(g, row_w, row_w)
    toep = exact("jrd,ljrc->ljdc", lane, by_block(toep))
    toep = exact("ljrc,jcd->ljrd", toep, lane).reshape(g, row_w, row_w).astype(BF16)
    zpad = jnp.zeros((g, row_w, LANES - p), F32)
    binj = jnp.concatenate([bjr, zpad, bji, zpad], axis=-1)
    binj = exact("jrd,ljrc->ljdc", lane, by_block(binj)).reshape(g, row_w, 2 * LANES).astype(BF16)
    zrow = jnp.zeros((g, LANES - p, row_w), F32)
    ct = jnp.concatenate([cor.transpose(0, 2, 1), zrow, -coi.transpose(0, 2, 1), zrow], axis=1)
    ct = exact("ljrc,jcd->ljrd", by_block(ct), lane).reshape(g, 2 * LANES, row_w).astype(BF16)
    dtile = jnp.tile(d_skip.astype(F32).reshape(g, 1, hs), (1, 1, nt))
    lpad = jnp.zeros((g, 1, LANES - p), F32)
    l16r = jnp.concatenate([l16r, lpad], axis=-1)
    l16i = jnp.concatenate([l16i, lpad], axis=-1)

    per_g3 = lambda i: (i, 0, 0)
    return pl.pallas_call(
        functools.partial(_s5_kernel, streams=streams),
        grid=(g,),
        in_specs=[pl.BlockSpec((1, rows, nt * hs), per_g3),
                  pl.BlockSpec((1, nt * hs, nt * hs), per_g3),
                  pl.BlockSpec((1, nt * hs, 2 * LANES), per_g3),
                  pl.BlockSpec((1, 2 * LANES, nt * hs), per_g3),
                  pl.BlockSpec((1, 1, nt * hs), per_g3),
                  pl.BlockSpec((1, 1, LANES), per_g3),
                  pl.BlockSpec((1, 1, LANES), per_g3)],
        out_specs=pl.BlockSpec((1, rows, nt * hs), per_g3),
        out_shape=jax.ShapeDtypeStruct((g, rows, nt * hs), F32),
        scratch_shapes=[pltpu.VMEM((rows, LANES), F32)] * 4,
        compiler_params=_cparams("arbitrary"),
        name="s5_chunked_scan",
    )(ut, toep, binj, ct, dtile, l16r, l16i)


def _sb_kernel(q_ref, k_ref, v_ref, o_ref, acc_ref, run_ref, *, qb, kb, nsub):
    qi = pl.program_id(2)
    per_key = kb // qb
    pairs = nsub // per_key
    ri = lax.broadcasted_iota(jnp.int32, (kb, kb), 0)
    ci = lax.broadcasted_iota(jnp.int32, (kb, kb), 1)
    after = (ri > ci).astype(BF16)
    rq = lax.broadcasted_iota(jnp.int32, (qb, kb), 0)
    cq = lax.broadcasted_iota(jnp.int32, (qb, kb), 1)
    diag_valid = [cq < rq + r * qb for r in range(per_key)]
    base = pairs * qi

    def pre(sub, kblk, valid):
        s = _dot_nt(q_ref[0, 0, sub * qb:(sub + 1) * qb, :], kblk)
        lse = jnp.log2(1.0 + jnp.exp2(-jnp.abs(s)))
        log_p = jnp.minimum(s, 0.0) - lse
        neg_log_1m = jnp.maximum(s, 0.0) + lse
        if valid is not None:
            neg_log_1m = jnp.where(valid, neg_log_1m, 0.0)
        hi, lo = _split2(neg_log_1m)
        later = _dot(hi, after) + _dot(lo, after)
        return log_p - later, jnp.sum(neg_log_1m, axis=1, keepdims=True)

    def post(sub, part, tot, vblk, valid, exists):
        rows = slice(sub * qb, (sub + 1) * qb)
        run = run_ref[rows, :]
        if exists is not None:
            run = run + jnp.where(exists, 0.0, SB_NO_KEYS)
            tot = jnp.where(exists, tot, 0.0)
        wts = jnp.exp2(part - run)
        if valid is not None:
            wts = jnp.where(valid, wts, 0.0)
        acc_ref[rows, :] += _dot(wts.astype(BF16), vblk)
        run_ref[rows, :] = run + tot

    def band_pre(e, diagonal):
        items = []
        for m in range(pairs):
            idx = base + m - e
            exists = None if diagonal else idx >= 0
            k0 = pl.multiple_of(jnp.maximum(idx, 0) * kb, kb)
            kblk, vblk = k_ref[0, 0, pl.ds(k0, kb), :], v_ref[0, 0, pl.ds(k0, kb), :]
            for r in range(per_key):
                sub = m * per_key + r
                valid = diag_valid[r] if diagonal else None
                items.append((sub, vblk, valid, exists, pre(sub, kblk, valid)))
        return items

    def band_post(items):
        for sub, vblk, valid, exists, (part, tot) in items:
            post(sub, part, tot, vblk, valid, exists)

    acc_ref[...] = jnp.zeros(acc_ref.shape, F32)
    run_ref[...] = jnp.zeros(run_ref.shape, F32)
    first = [band_pre(e, e == 0) for e in range(SB_STATIC_BANDS)]
    for items in first:
        band_post(items)

    last = base + pairs - 1

    def keep_going(state):
        e, run_min = state
        return jnp.logical_and(e <= last, run_min < SB_UNDERFLOW_LOG2)

    def body(state):
        e, _ = state
        band_post(band_pre(e, False))
        return e + 1, jnp.min(run_ref[...])

    lax.while_loop(keep_going, body, (jnp.int32(SB_STATIC_BANDS), jnp.min(run_ref[...])))
    o_ref[0, 0] = acc_ref[...].astype(o_ref.dtype)


def _sb_attention(q, kv):
    bsz, heads, seq, d = q.shape
    kb = min(SB_KEY_BLOCK, seq)
    qb = min(SB_QUERY_BLOCK, kb)
    tq = min(SB_Q_TILE, seq)
    nsub = tq // qb
    return pl.pallas_call(
        functools.partial(_sb_kernel, qb=qb, kb=kb, nsub=nsub),
        grid=(bsz, heads, seq // tq),
        in_specs=[pl.BlockSpec((1, 1, tq, d), lambda b, h, i: (b, h, i, 0)),
                  pl.BlockSpec((1, 1, seq, d), lambda b, h, i: (b, h, 0, 0)),
                  pl.BlockSpec((1, 1, seq, d), lambda b, h, i: (b, heads + h, 0, 0))],
        out_specs=pl.BlockSpec((1, 1, tq, d), lambda b, h, i: (b, h, i, 0)),
        out_shape=jax.ShapeDtypeStruct((bsz, heads, seq, d), BF16),
        scratch_shapes=[pltpu.VMEM((tq, d), F32), pltpu.VMEM((tq, 1), F32)],
        compiler_params=_cparams("arbitrary", "arbitrary", "arbitrary"),
        name="stick_breaking_attention",
    )(q, kv, kv)


def _gelu_tanh(x):
    return 0.5 * x * (1.0 + jnp.tanh(0.7978845608028654 * (x + 0.044715 * (x * x * x))))


def _mix_ffn_kernel(*refs, even, final, f_chunks):
    if even:
        (h_ref, o_ref, y_ref, gw_ref, gb_ref, wtop_ref, wbot_ref, gtm_ref, nw_ref, sc_ref, sh_ref, gtf_ref,
         win_ref, wout_ref, fw_ref, out_ref, ybuf_ref) = refs
        y = _gelu_tanh(_from_chunk_major(y_ref, ybuf_ref))
        y = y * _sigmoid(_dot(y.astype(BF16), gw_ref[...]) + gb_ref[...])
        mix = _dot(o_ref[0], wtop_ref[...]) + _dot(y.astype(BF16), wbot_ref[...])
    else:
        (h_ref, o_ref, wtop_ref, gtm_ref, nw_ref, sc_ref, sh_ref, gtf_ref,
         win_ref, wout_ref, fw_ref, out_ref) = refs
        o = jnp.concatenate([o_ref[0, hh] for hh in range(o_ref.shape[1])], axis=1)
        mix = _dot(o, wtop_ref[...])
    h1 = h_ref[0] + gtm_ref[0] * mix
    hn = _norm_mod(h1, nw_ref[...], sc_ref[0], sh_ref[0]).astype(BF16)
    hidden = wout_ref.shape[0]
    fc = hidden // f_chunks
    acc = None
    for c in range(f_chunks):
        gate = _dot(hn, win_ref[:, c * fc:(c + 1) * fc])
        up = _dot(hn, win_ref[:, hidden + c * fc:hidden + (c + 1) * fc])
        part = _dot((_silu(gate) * up).astype(BF16), wout_ref[c * fc:(c + 1) * fc, :])
        acc = part if acc is None else acc + part
    h2 = h1 + gtf_ref[0] * acc
    if final:
        h2 = _rms(h2) * fw_ref[...]
    out_ref[0] = h2


def _mix_ffn(h, acts, mix_w, glu, gt_m, nw, sc, sh, gt_f, w_in, w_out, final_w, *, even, final):
    bsz, seq, d = h.shape
    tm = min(ROW_TILE, seq)
    hidden = w_out.shape[0]
    row = lambda b, t: (b, t, 0)
    per_b = lambda b, t: (b, 0, 0)
    vec_d = pl.BlockSpec((1, d), lambda b, t: (0, 0))
    mod_spec = pl.BlockSpec((1, 1, d), per_b)
    args, specs, scratch = [h], [pl.BlockSpec((1, tm, d), row)], []
    if even:
        o, y = acts
        half = o.shape[-1]
        nt = seq // tm
        glu_w, glu_b = glu
        args += [o, y, glu_w.astype(BF16), glu_b.astype(F32).reshape(1, -1),
                 mix_w[:half].astype(BF16), mix_w[half:].astype(BF16)]
        specs += [pl.BlockSpec((1, tm, half), row),
                  pl.BlockSpec((y.shape[0], tm // S5_CHUNK, y.shape[2]), lambda b, t: (0, b * nt + t, 0)),
                  _const_spec(glu_w.shape), pl.BlockSpec((1, glu_b.shape[-1]), lambda b, t: (0, 0)),
                  _const_spec((half, d)), _const_spec((d - half, d))]
        scratch = [pltpu.VMEM(((d - half) // LANES, tm, LANES), F32)]
    else:
        (o,) = acts
        args += [o, mix_w.astype(BF16)]
        specs += [pl.BlockSpec((1, o.shape[1], tm, o.shape[3]), lambda b, t: (b, 0, t, 0)), _const_spec((d, d))]
    args += [gt_m, nw.astype(F32).reshape(1, d), sc, sh, gt_f, w_in.astype(BF16), w_out.astype(BF16),
             final_w.astype(F32).reshape(1, d)]
    specs += [mod_spec, vec_d, mod_spec, mod_spec, mod_spec, _const_spec(w_in.shape), _const_spec(w_out.shape), vec_d]
    return pl.pallas_call(
        functools.partial(_mix_ffn_kernel, even=even, final=final, f_chunks=2),
        grid=(bsz, seq // tm),
        in_specs=specs,
        out_specs=pl.BlockSpec((1, tm, d), row),
        out_shape=jax.ShapeDtypeStruct((bsz, seq, d), F32),
        scratch_shapes=scratch,
        compiler_params=_cparams("arbitrary", "arbitrary"),
        name="outproj_swiglu_ffn",
    )(*args)


def kernel(x, c, ada_w, ada_b, norm_mix_w, norm_ffn_w, ffn_w_in, ffn_w_out, hy_w_in, hy_conv_w, hy_a_log, hy_dt_bias, hy_head_norm_w, s5_lam_re, s5_lam_im, s5_log_dt, s5_b_re, s5_b_im, s5_c_re, s5_c_im, s5_d, s5_glu_w, s5_glu_b, hy_w_out, sb_w_in, sb_w_out, final_norm_w):
    bsz, seq, d = x.shape
    depth = ada_w.shape[0]
    gdn_width = d // 2
    gdn_heads = gdn_width // GDN_HEAD_DIM
    s5_width = d - gdn_width

    mod = _modulation(c, ada_w, ada_b)
    h = x
    for i in range(depth):
        sh_m, sc_m, gt_m, sh_f, sc_f, gt_f = (mod[i, :bsz, k * d:(k + 1) * d].reshape(bsz, 1, d) for k in range(6))
        j = i // 2
        last = i == depth - 1
        if i % 2 == 0:
            w = hy_w_in[j]
            w4 = 4 * gdn_width
            gate_cols = jnp.pad(w[:, w4:w4 + 2 * gdn_heads], ((0, 0), (0, LANES - 2 * gdn_heads)))
            w_cat = jnp.concatenate([w[:, :w4], gate_cols, w[:, w4 + 2 * gdn_heads:]], axis=1).astype(BF16)
            qkv_pre, zg, ba, ut = _inproj(h, norm_mix_w[i], sc_m, sh_m, w_cat,
                                          (3 * gdn_width, gdn_width, LANES, s5_width), (F32, F32, F32, F32),
                                          chunk_major_last=True)
            o = _gdn(qkv_pre, zg, ba, hy_conv_w[j], hy_a_log[j], hy_dt_bias[j], hy_head_norm_w[j])
            y = _s5(ut, bsz, s5_lam_re[j], s5_lam_im[j], s5_log_dt[j], s5_b_re[j], s5_b_im[j],
                    s5_c_re[j], s5_c_im[j], s5_d[j])
            h = _mix_ffn(h, (o, y), hy_w_out[j], (s5_glu_w[j], s5_glu_b[j]), gt_m, norm_ffn_w[i], sc_f, sh_f, gt_f,
                         ffn_w_in[i], ffn_w_out[i], final_norm_w, even=True, final=last)
        else:
            q, kv = _inproj(h, norm_mix_w[i], sc_m, sh_m, sb_w_in[j].astype(BF16), (d, 2 * d), (BF16, BF16),
                            scales=(SB_HEAD_DIM ** -0.5 * LOG2_E, 1.0), head_dim=SB_HEAD_DIM)
            o = _sb_attention(q, kv)
            h = _mix_ffn(h, (o,), sb_w_out[j], None, gt_m, norm_ffn_w[i], sc_f, sh_f, gt_f,
                         ffn_w_in[i], ffn_w_out[i], final_norm_w, even=False, final=last)
    return h
```

```python
import functools

import jax
import jax.numpy as jnp
import numpy as np
from jax import lax
from jax.experimental import pallas as pl
from jax.experimental.pallas import tpu as pltpu

F32 = jnp.float32
BF16 = jnp.bfloat16
NORM_EPS = 1e-6

GDN_HEAD_DIM = 128
CONV_K = 4
GDN_CHUNK = 64
S5_GROUP = 16
S5_STATE = 64
SB_HEAD_DIM = 128

LANES = 128
SUBLANES = 8
VMEM_LIMIT_BYTES = 56 * 1024 * 1024

ROW_TILE = 512
GDN_TILE = 256
S5_CHUNK = 16
SB_KEY_BLOCK = 256
SB_QUERY_BLOCK = 256
SB_Q_TILE = 2048
SB_STATIC_BANDS = 2
SB_NO_KEYS = 1.0e6
LOG2_E = 1.4426950408889634
SB_UNDERFLOW_LOG2 = 160.0
MOD_COL_TILE = 1536


def _cparams(*sem):
    return pltpu.CompilerParams(dimension_semantics=sem, vmem_limit_bytes=VMEM_LIMIT_BYTES)


def _const_spec(shape):
    nd = len(shape)
    return pl.BlockSpec(shape, lambda *_: (0,) * nd, pipeline_mode=pl.Buffered(1))


def _dot(a, b):
    return jnp.dot(a, b, preferred_element_type=F32)


def _dot_nt(a, b):
    return lax.dot_general(a, b, (((1,), (1,)), ((), ())), preferred_element_type=F32)


def _dot_tn(a, b):
    return lax.dot_general(a, b, (((0,), (0,)), ((), ())), preferred_element_type=F32)


def _split2(x):
    hi = x.astype(BF16)
    lo = (x - hi.astype(F32)).astype(BF16)
    return hi, lo


def _mm3(a, b):
    ah, al = _split2(a)
    bh, bl = _split2(b)
    return _dot(ah, bh) + (_dot(ah, bl) + _dot(al, bh))


def _mm3_nt(a, b):
    ah, al = _split2(a)
    bh, bl = _split2(b)
    return _dot_nt(ah, bh) + (_dot_nt(ah, bl) + _dot_nt(al, bh))


def _sigmoid(x):
    return 1.0 / (1.0 + jnp.exp(-x))


def _silu(x):
    return x * _sigmoid(x)


def _softplus(x):
    return jnp.maximum(x, 0.0) + jnp.log(1.0 + jnp.exp(-jnp.abs(x)))


def _rms(x):
    return x * lax.rsqrt(jnp.mean(x * x, axis=-1, keepdims=True) + NORM_EPS)


def _norm_mod(x, nw, sc, sh):
    return (_rms(x) * nw) * (1.0 + sc) + sh


def _mod_kernel(c_ref, w_ref, b_ref, o_ref):
    ca = _silu(c_ref[...])
    o_ref[0] = _mm3(ca, w_ref[0]) + b_ref[0]


def _modulation(c, ada_w, ada_b):
    depth, d, n = ada_w.shape
    bsz = c.shape[0]
    rows = -(-bsz // SUBLANES) * SUBLANES
    c_pad = jnp.pad(c, ((0, rows - bsz), (0, 0)))
    tn = MOD_COL_TILE
    return pl.pallas_call(
        _mod_kernel,
        grid=(depth, n // tn),
        in_specs=[pl.BlockSpec((rows, d), lambda i, j: (0, 0)),
                  pl.BlockSpec((1, d, tn), lambda i, j: (i, 0, j)),
                  pl.BlockSpec((1, 1, tn), lambda i, j: (i, 0, j))],
        out_specs=pl.BlockSpec((1, rows, tn), lambda i, j: (i, 0, j)),
        out_shape=jax.ShapeDtypeStruct((depth, rows, n), F32),
        compiler_params=_cparams("arbitrary", "arbitrary"),
        name="adaln_modulation",
    )(c_pad, ada_w, ada_b.reshape(depth, 1, n))


def _s5_lane(group, s, hi=0):
    per_block = LANES // S5_GROUP
    return (s // per_block) * LANES + ((s + group) % per_block) * S5_GROUP + hi


def _to_chunk_major(x, z_ref, buf_ref):
    tokens, width = x.shape
    chunks = tokens // S5_CHUNK
    per_block = LANES // S5_GROUP
    for lb in range(width // LANES):
        buf_ref[lb] = x[:, lb * LANES:(lb + 1) * LANES]
    for s in range(S5_CHUNK):
        shift = (s % per_block) * S5_GROUP
        for lb in range(width // LANES):
            rows = buf_ref[lb, pl.ds(s, chunks, stride=S5_CHUNK), :]
            rows = pltpu.roll(rows, shift, axis=1) if shift else rows
            for j in range(per_block):
                src = ((s + j) % per_block) * S5_GROUP
                dst = _s5_lane(lb * per_block + j, s)
                z_ref[lb * per_block + j, :, dst:dst + S5_GROUP] = rows[:, src:src + S5_GROUP]


def _from_chunk_major(z_ref, buf_ref):
    groups, chunks, _ = z_ref.shape
    per_block = LANES // S5_GROUP
    nlb = groups // per_block
    lane_block = lax.broadcasted_iota(jnp.int32, (chunks, LANES), 1) // S5_GROUP
    for s in range(S5_CHUNK):
        half = s // per_block
        shift = (s % per_block) * S5_GROUP
        for lb in range(nlb):
            rows = None
            for j in range(per_block):
                piece = z_ref[lb * per_block + j, :, half * LANES:(half + 1) * LANES]
                rows = piece if rows is None else jnp.where(lane_block == (s + j) % per_block, piece, rows)
            rows = pltpu.roll(rows, LANES - shift, axis=1) if shift else rows
            buf_ref[lb, pl.ds(s, chunks, stride=S5_CHUNK), :] = rows
    return jnp.concatenate([buf_ref[lb] for lb in range(nlb)], axis=1)


def _inproj_kernel(h_ref, nw_ref, sc_ref, sh_ref, w_ref, *refs, widths, scales, chunk_major_last, head_dim):
    out_refs = refs[:len(widths)]
    hn = _norm_mod(h_ref[0], nw_ref[...], sc_ref[0], sh_ref[0]).astype(BF16)
    off = 0
    for idx, (o_ref, wd, s) in enumerate(zip(out_refs, widths, scales)):
        acc = _dot(hn, w_ref[:, off:off + wd])
        if s != 1.0:
            acc = acc * s
        if chunk_major_last and idx == len(widths) - 1:
            _to_chunk_major(acc, o_ref, refs[len(widths)])
        elif head_dim:
            for hh in range(wd // head_dim):
                o_ref[0, hh] = acc[:, hh * head_dim:(hh + 1) * head_dim].astype(o_ref.dtype)
        else:
            o_ref[0] = acc.astype(o_ref.dtype)
        off += wd


def _inproj(h, nw, sc, sh, w, widths, dtypes, scales=None, chunk_major_last=False, head_dim=0):
    scales = scales or (1.0,) * len(widths)
    bsz, seq, d = h.shape
    tm = min(ROW_TILE, seq)
    nt = seq // tm
    n = w.shape[1]
    row = lambda b, t: (b, t, 0)
    per_b = lambda b, t: (b, 0, 0)
    if head_dim:
        out_specs = [pl.BlockSpec((1, wd // head_dim, tm, head_dim), lambda b, t: (b, 0, t, 0)) for wd in widths]
        out_shape = [jax.ShapeDtypeStruct((bsz, wd // head_dim, seq, head_dim), dt) for wd, dt in zip(widths, dtypes)]
    else:
        out_specs = [pl.BlockSpec((1, tm, wd), row) for wd in widths]
        out_shape = [jax.ShapeDtypeStruct((bsz, seq, wd), dt) for wd, dt in zip(widths, dtypes)]
    scratch = []
    if chunk_major_last:
        groups = widths[-1] // S5_GROUP
        row_w = S5_CHUNK * S5_GROUP
        out_specs[-1] = pl.BlockSpec((groups, tm // S5_CHUNK, row_w), lambda b, t: (0, b * nt + t, 0))
        out_shape[-1] = jax.ShapeDtypeStruct((groups, bsz * seq // S5_CHUNK, row_w), dtypes[-1])
        scratch = [pltpu.VMEM((widths[-1] // LANES, tm, LANES), F32)]
    return pl.pallas_call(
        functools.partial(_inproj_kernel, widths=widths, scales=scales, chunk_major_last=chunk_major_last,
                          head_dim=head_dim),
        grid=(bsz, nt),
        in_specs=[pl.BlockSpec((1, tm, d), row),
                  pl.BlockSpec((1, d), lambda b, t: (0, 0)),
                  pl.BlockSpec((1, 1, d), per_b),
                  pl.BlockSpec((1, 1, d), per_b),
                  _const_spec((d, n))],
        out_specs=out_specs,
        out_shape=out_shape,
        scratch_shapes=scratch,
        compiler_params=_cparams("arbitrary", "arbitrary"),
        name="norm_mod_inproj",
    )(h, nw.reshape(1, d), sc, sh, w)


def _unit_lower_inverses(mats, ri, ci):
    n = mats[0].shape[0]
    eye = (ri == ci).astype(F32)
    pair = (ri // 2 == ci // 2) & (ri > ci)
    ts = [eye - jnp.where(pair, a, 0.0) for a in mats]
    s = 2
    while s < n:
        sel = (ri // (2 * s) == ci // (2 * s)) & (ri // s > ci // s)
        tmps = [_mm3(t, jnp.where(sel, a, 0.0)) for t, a in zip(ts, mats)]
        ts = [t - _mm3(tmp, t) for t, tmp in zip(ts, tmps)]
        s *= 2
    return ts


def _gdn_kernel(x_ref, z_ref, ba_ref, cw_ref, garow_ref, gbrow_ref, hnw_ref, o_ref,
                xbuf_ref, state_ref, *, heads, tile):
    dh = GDN_HEAD_DIM
    width = heads * dh
    ck = GDN_CHUNK
    nck = tile // ck
    t_id = pl.program_id(1)

    @pl.when(t_id == 0)
    def _():
        xbuf_ref[0:SUBLANES, :] = jnp.zeros((SUBLANES, 3 * width), F32)
        state_ref[...] = jnp.zeros(state_ref.shape, F32)

    x = x_ref[0]
    xbuf_ref[SUBLANES:SUBLANES + tile, :] = x
    conv = cw_ref[0:1, :] * xbuf_ref[pl.ds(SUBLANES - CONV_K + 1, tile), :]
    for k in range(1, CONV_K):
        conv = conv + cw_ref[k:k + 1, :] * xbuf_ref[pl.ds(SUBLANES - CONV_K + 1 + k, tile), :]
    xbuf_ref[0:SUBLANES, :] = x[tile - SUBLANES:tile, :]
    qkv = _silu(conv)

    ba = ba_ref[0]
    beta_all = _sigmoid(ba)
    g_all = -jnp.exp(garow_ref[...]) * _softplus(ba + gbrow_ref[...])
    ri_t = lax.broadcasted_iota(jnp.int32, (tile, tile), 0)
    ci_t = lax.broadcasted_iota(jnp.int32, (tile, tile), 1)
    cum_mat = ((ri_t // ck == ci_t // ck) & (ri_t >= ci_t)).astype(BF16)
    g_hi = g_all.astype(BF16)
    g_r1 = g_all - g_hi.astype(F32)
    g_mid = g_r1.astype(BF16)
    g_lo = (g_r1 - g_mid.astype(F32)).astype(BF16)
    gc_all = _dot(cum_mat, g_hi) + (_dot(cum_mat, g_mid) + _dot(cum_mat, g_lo))
    gc_all_t = gc_all.T

    ri = lax.broadcasted_iota(jnp.int32, (ck, ck), 0)
    ci = lax.broadcasted_iota(jnp.int32, (ck, ck), 1)
    hnw = hnw_ref[...]
    z_all = z_ref[0]

    prep = []
    for n in range(nck):
        sl = slice(n * ck, (n + 1) * ck)
        for h in range(heads):
            q = qkv[sl, h * dh:(h + 1) * dh]
            k = qkv[sl, width + h * dh:width + (h + 1) * dh]
            v = qkv[sl, 2 * width + h * dh:2 * width + (h + 1) * dh]
            q = q * lax.rsqrt(jnp.sum(q * q, axis=-1, keepdims=True) + NORM_EPS) * (dh ** -0.5)
            k = k * lax.rsqrt(jnp.sum(k * k, axis=-1, keepdims=True) + NORM_EPS)
            beta = beta_all[sl, h:h + 1]
            gcc = gc_all[sl, heads + h:heads + h + 1]
            gcr = gc_all_t[heads + h:heads + h + 1, sl]
            g_last = gcc[ck - 1:ck, :]
            decay = jnp.where(ri >= ci, jnp.exp(jnp.minimum(gcc - gcr, 0.0)), 0.0)
            k_beta = k * beta
            kb16, k16 = k_beta.astype(BF16), k.astype(BF16)
            egc = jnp.exp(gcc)
            prep.append(dict(
                a=jnp.where(ri > ci, _dot_nt(kb16, k16) * decay, 0.0),
                rhs=jnp.concatenate([v * beta, k_beta * egc], axis=-1),
                attn=(_dot_nt(q.astype(BF16), k16) * decay).astype(BF16),
                q_dec=(q * egc).astype(BF16),
                k_dec=(k * jnp.exp(g_last - gcc)).astype(BF16),
                s_decay=jnp.exp(g_last)))
    invs = _unit_lower_inverses([p["a"] for p in prep], ri, ci)
    sols = [_mm3(t, p["rhs"]) for t, p in zip(invs, prep)]

    states = [state_ref[h] for h in range(heads)]
    outs = [[] for _ in range(heads)]
    for n in range(nck):
        cur = [(prep[n * heads + h], sols[n * heads + h]) for h in range(heads)]
        s16 = [st.astype(BF16) for st in states]
        vn16 = [(sol[:, :dh] - _dot(sol[:, dh:].astype(BF16), s)).astype(BF16) for (_, sol), s in zip(cur, s16)]
        for h in range(heads):
            p = cur[h][0]
            outs[h].append(_dot(p["q_dec"], s16[h]) + _dot(p["attn"], vn16[h]))
        states = [st * cur[h][0]["s_decay"] + _dot_tn(cur[h][0]["k_dec"], vn16[h]) for h, st in enumerate(states)]
    for h in range(heads):
        state_ref[h] = states[h]
        o_h = jnp.concatenate(outs[h], axis=0) if nck > 1 else outs[h][0]
        o_h = (_rms(o_h) * hnw) * _silu(z_all[:, h * dh:(h + 1) * dh])
        o_ref[0, :, h * dh:(h + 1) * dh] = o_h.astype(o_ref.dtype)


def _gdn(qkv_pre, z, ba, conv_w, a_log, dt_bias, head_norm_w):
    bsz, seq, w3 = qkv_pre.shape
    width = w3 // 3
    heads = width // GDN_HEAD_DIM
    tile = min(GDN_TILE, seq)
    garow = jnp.zeros((1, LANES), F32).at[0, heads:2 * heads].set(a_log.astype(F32))
    gbrow = jnp.zeros((1, LANES), F32).at[0, heads:2 * heads].set(dt_bias.astype(F32))
    row = lambda b, t: (b, t, 0)
    fix2 = lambda b, t: (0, 0)
    return pl.pallas_call(
        functools.partial(_gdn_kernel, heads=heads, tile=tile),
        grid=(bsz, seq // tile),
        in_specs=[pl.BlockSpec((1, tile, w3), row),
                  pl.BlockSpec((1, tile, width), row),
                  pl.BlockSpec((1, tile, LANES), row),
                  pl.BlockSpec((CONV_K, w3), fix2),
                  pl.BlockSpec((1, LANES), fix2),
                  pl.BlockSpec((1, LANES), fix2),
                  pl.BlockSpec((1, GDN_HEAD_DIM), fix2)],
        out_specs=pl.BlockSpec((1, tile, width), row),
        out_shape=jax.ShapeDtypeStruct((bsz, seq, width), BF16),
        scratch_shapes=[pltpu.VMEM((SUBLANES + tile, w3), F32),
                        pltpu.VMEM((heads, GDN_HEAD_DIM, GDN_HEAD_DIM), F32)],
        compiler_params=_cparams("arbitrary", "arbitrary"),
        name="gated_deltanet",
    )(qkv_pre, z, ba, conv_w.astype(F32), garow, gbrow, head_norm_w.astype(F32).reshape(1, -1))


def _s5_prep_kernel(lr_ref, li_ref, ldt_ref, btr_ref, bti_ref, cr_ref, ci_ref,
                    kst_ref, bjr_ref, bji_ref, cor_ref, coi_ref, l16r_ref, l16i_ref):
    nt = S5_CHUNK
    lr, li = lr_ref[0], li_ref[0]
    dt = jnp.exp(ldt_ref[0])
    mag = jnp.exp(lr * dt)
    lb_re, lb_im = mag * jnp.cos(li * dt), mag * jnp.sin(li * dt)
    den = lr * lr + li * li
    num_re, num_im = lb_re - 1.0, lb_im
    f_re = (num_re * lr + num_im * li) / den
    f_im = (num_im * lr - num_re * li) / den
    btr, bti = btr_ref[0], bti_ref[0]
    bb_re = f_re * btr - f_im * bti
    bb_im = f_re * bti + f_im * btr
    cr, ci = cr_ref[0], ci_ref[0]
    p_re, p_im = [jnp.ones_like(lr)], [jnp.zeros_like(lr)]
    for _ in range(nt):
        p_re.append(p_re[-1] * lb_re - p_im[-1] * lb_im)
        p_im.append(p_re[-2] * lb_im + p_im[-1] * lb_re)
    cp_re = [cr * pr - ci * pi for pr, pi in zip(p_re, p_im)]
    cp_im = [cr * pi + ci * pr for pr, pi in zip(p_re, p_im)]
    kst_ref[0] = (_mm3_nt(jnp.concatenate(cp_re[:nt], axis=0), bb_re)
                  - _mm3_nt(jnp.concatenate(cp_im[:nt], axis=0), bb_im))
    bjr_ref[0] = jnp.concatenate([bb_re * p_re[nt - 1 - s] - bb_im * p_im[nt - 1 - s] for s in range(nt)], axis=0)
    bji_ref[0] = jnp.concatenate([bb_re * p_im[nt - 1 - s] + bb_im * p_re[nt - 1 - s] for s in range(nt)], axis=0)
    cor_ref[0] = jnp.concatenate(cp_re[1:nt + 1], axis=0)
    coi_ref[0] = jnp.concatenate(cp_im[1:nt + 1], axis=0)
    l16r_ref[0] = p_re[nt]
    l16i_ref[0] = p_im[nt]


def _s5_prep(lam_re, lam_im, log_dt, b_re, b_im, c_re, c_im):
    g, p = lam_re.shape
    hs = c_re.shape[1]
    nt = S5_CHUNK
    vec = pl.BlockSpec((1, 1, p), lambda i: (i, 0, 0))
    mat = pl.BlockSpec((1, hs, p), lambda i: (i, 0, 0))
    big = pl.BlockSpec((1, nt * hs, p), lambda i: (i, 0, 0))
    f32 = lambda a: a.astype(F32)
    return pl.pallas_call(
        _s5_prep_kernel,
        grid=(g,),
        in_specs=[vec, vec, pl.BlockSpec((1, 1, 1), lambda i: (i, 0, 0)), mat, mat, mat, mat],
        out_specs=[pl.BlockSpec((1, nt * hs, hs), lambda i: (i, 0, 0)), big, big, big, big, vec, vec],
        out_shape=[jax.ShapeDtypeStruct((g, nt * hs, hs), F32)]
                  + [jax.ShapeDtypeStruct((g, nt * hs, p), F32)] * 4
                  + [jax.ShapeDtypeStruct((g, 1, p), F32)] * 2,
        compiler_params=_cparams("arbitrary"),
        name="s5_discretise",
    )(f32(lam_re).reshape(g, 1, p), f32(lam_im).reshape(g, 1, p), f32(log_dt).reshape(g, 1, 1),
      f32(b_re).transpose(0, 2, 1), f32(b_im).transpose(0, 2, 1), f32(c_re), f32(c_im))


def _s5_kernel(u_ref, toep_ref, binj_ref, ct_ref, dt_ref, l16r_ref, l16i_ref, y_ref,
               sre_ref, sim_ref, xre_ref, xim_ref, *, streams):
    rows = u_ref.shape[1]
    steps = rows // streams
    u = u_ref[0]
    ub = u.astype(BF16)
    inj = _dot(ub, binj_ref[0])
    sre_ref[...] = inj[:, :LANES]
    sim_ref[...] = inj[:, LANES:]
    lr = jnp.broadcast_to(l16r_ref[0], (streams, LANES))
    li = jnp.broadcast_to(l16i_ref[0], (streams, LANES))

    def scan_step(i, carry):
        xr, xi = carry
        at = pl.ds(i, streams, stride=steps)
        xre_ref[at, :] = xr
        xim_ref[at, :] = xi
        sr = sre_ref[at, :]
        si = sim_ref[at, :]
        return lr * xr - li * xi + sr, lr * xi + li * xr + si

    zero = jnp.zeros((streams, LANES), F32)
    lax.fori_loop(0, steps, scan_step, (zero, zero), unroll=8)

    xin = jnp.concatenate([xre_ref[...], xim_ref[...]], axis=1).astype(BF16)
    y_ref[0] = _dot(ub, toep_ref[0]) + _dot(xin, ct_ref[0]) + dt_ref[0] * u


def _s5(ut, bsz, lam_re, lam_im, log_dt, b_re, b_im, c_re, c_im, d_skip):
    g, rows, _ = ut.shape
    hs, p, nt = S5_GROUP, S5_STATE, S5_CHUNK
    streams = bsz
    assert p * 2 == LANES and nt * hs == 2 * LANES
    steps = rows // streams
    kst, bjr, bji, cor, coi, l16r, l16i = _s5_prep(lam_re, lam_im, log_dt, b_re, b_im, c_re, c_im)

    per_block = LANES // hs
    row_w = nt * hs
    place = np.zeros((nt, nt, nt), np.float32)
    lane = np.zeros((per_block, row_w, row_w), np.float32)
    for s in range(nt):
        for t in range(s, nt):
            place[t - s, s, t] = 1.0
        for j in range(per_block):
            for hi in range(hs):
                lane[j, s * hs + hi, _s5_lane(j, s, hi)] = 1.0
    exact = functools.partial(jnp.einsum, precision=lax.Precision.HIGHEST)
    by_block = lambda a: a.reshape((g // per_block, per_block) + a.shape[1:])
    k4 = kst.reshape(g, nt, hs, hs)
    toep = exact("xst,gxoh->gshto", place, k4).reshape(g, row_w, row_w)
    toep = exact("jrd,ljrc->ljdc", lane, by_block(toep))
    toep = exact("ljrc,jcd->ljrd", toep, lane).reshape(g, row_w, row_w).astype(BF16)
    zpad = jnp.zeros((g, row_w, LANES - p), F32)
    binj = jnp.concatenate([bjr, zpad, bji, zpad], axis=-1)
    binj = exact("jrd,ljrc->ljdc", lane, by_block(binj)).reshape(g, row_w, 2 * LANES).astype(BF16)
    zrow = jnp.zeros((g, LANES - p, row_w), F32)
    ct = jnp.concatenate([cor.transpose(0, 2, 1), zrow, -coi.transpose(0, 2, 1), zrow], axis=1)
    ct = exact("ljrc,jcd->ljrd", by_block(ct), lane).reshape(g, 2 * LANES, row_w).astype(BF16)
    dtile = jnp.tile(d_skip.astype(F32).reshape(g, 1, hs), (1, 1, nt))
    lpad = jnp.zeros((g, 1, LANES - p), F32)
    l16r = jnp.concatenate([l16r, lpad], axis=-1)
    l16i = jnp.concatenate([l16i, lpad], axis=-1)

    per_g3 = lambda i: (i, 0, 0)
    return pl.pallas_call(
        functools.partial(_s5_kernel, streams=streams),
        grid=(g,),
        in_specs=[pl.BlockSpec((1, rows, nt * hs), per_g3),
                  pl.BlockSpec((1, nt * hs, nt * hs), per_g3),
                  pl.BlockSpec((1, nt * hs, 2 * LANES), per_g3),
                  pl.BlockSpec((1, 2 * LANES, nt * hs), per_g3),
                  pl.BlockSpec((1, 1, nt * hs), per_g3),
                  pl.BlockSpec((1, 1, LANES), per_g3),
                  pl.BlockSpec((1, 1, LANES), per_g3)],
        out_specs=pl.BlockSpec((1, rows, nt * hs), per_g3),
        out_shape=jax.ShapeDtypeStruct((g, rows, nt * hs), F32),
        scratch_shapes=[pltpu.VMEM((rows, LANES), F32)] * 4,
        compiler_params=_cparams("arbitrary"),
        name="s5_chunked_scan",
    )(ut, toep, binj, ct, dtile, l16r, l16i)


def _sb_kernel(q_ref, k_ref, v_ref, o_ref, acc_ref, run_ref, *, qb, kb, nsub):
    qi = pl.program_id(2)
    per_key = kb // qb
    pairs = nsub // per_key
    ri = lax.broadcasted_iota(jnp.int32, (kb, kb), 0)
    ci = lax.broadcasted_iota(jnp.int32, (kb, kb), 1)
    after = (ri > ci).astype(BF16)
    rq = lax.broadcasted_iota(jnp.int32, (qb, kb), 0)
    cq = lax.broadcasted_iota(jnp.int32, (qb, kb), 1)
    diag_valid = [cq < rq + r * qb for r in range(per_key)]
    base = pairs * qi

    def pre(sub, kblk, valid):
        s = _dot_nt(q_ref[0, 0, sub * qb:(sub + 1) * qb, :], kblk)
        lse = jnp.log2(1.0 + jnp.exp2(-jnp.abs(s)))
        log_p = jnp.minimum(s, 0.0) - lse
        neg_log_1m = jnp.maximum(s, 0.0) + lse
        if valid is not None:
            neg_log_1m = jnp.where(valid, neg_log_1m, 0.0)
        hi, lo = _split2(neg_log_1m)
        later = _dot(hi, after) + _dot(lo, after)
        return log_p - later, jnp.sum(neg_log_1m, axis=1, keepdims=True)

    def post(sub, part, tot, vblk, valid, exists):
        rows = slice(sub * qb, (sub + 1) * qb)
        run = run_ref[rows, :]
        if exists is not None:
            run = run + jnp.where(exists, 0.0, SB_NO_KEYS)
            tot = jnp.where(exists, tot, 0.0)
        wts = jnp.exp2(part - run)
        if valid is not None:
            wts = jnp.where(valid, wts, 0.0)
        acc_ref[rows, :] += _dot(wts.astype(BF16), vblk)
        run_ref[rows, :] = run + tot

    def band_pre(e, diagonal):
        items = []
        for m in range(pairs):
            idx = base + m - e
            exists = None if diagonal else idx >= 0
            k0 = pl.multiple_of(jnp.maximum(idx, 0) * kb, kb)
            kblk, vblk = k_ref[0, 0, pl.ds(k0, kb), :], v_ref[0, 0, pl.ds(k0, kb), :]
            for r in range(per_key):
                sub = m * per_key + r
                valid = diag_valid[r] if diagonal else None
                items.append((sub, vblk, valid, exists, pre(sub, kblk, valid)))
        return items

    def band_post(items):
        for sub, vblk, valid, exists, (part, tot) in items:
            post(sub, part, tot, vblk, valid, exists)

    acc_ref[...] = jnp.zeros(acc_ref.shape, F32)
    run_ref[...] = jnp.zeros(run_ref.shape, F32)
    first = [band_pre(e, e == 0) for e in range(SB_STATIC_BANDS)]
    for items in first:
        band_post(items)

    last = base + pairs - 1

    def keep_going(state):
        e, run_min = state
        return jnp.logical_and(e <= last, run_min < SB_UNDERFLOW_LOG2)

    def body(state):
        e, _ = state
        band_post(band_pre(e, False))
        return e + 1, jnp.min(run_ref[...])

    lax.while_loop(keep_going, body, (jnp.int32(SB_STATIC_BANDS), jnp.min(run_ref[...])))
    o_ref[0, 0] = acc_ref[...].astype(o_ref.dtype)


def _sb_attention(q, kv):
    bsz, heads, seq, d = q.shape
    kb = min(SB_KEY_BLOCK, seq)
    qb = min(SB_QUERY_BLOCK, kb)
    tq = min(SB_Q_TILE, seq)
    nsub = tq // qb
    return pl.pallas_call(
        functools.partial(_sb_kernel, qb=qb, kb=kb, nsub=nsub),
        grid=(bsz, heads, seq // tq),
        in_specs=[pl.BlockSpec((1, 1, tq, d), lambda b, h, i: (b, h, i, 0)),
                  pl.BlockSpec((1, 1, seq, d), lambda b, h, i: (b, h, 0, 0)),
                  pl.BlockSpec((1, 1, seq, d), lambda b, h, i: (b, heads + h, 0, 0))],
        out_specs=pl.BlockSpec((1, 1, tq, d), lambda b, h, i: (b, h, i, 0)),
        out_shape=jax.ShapeDtypeStruct((bsz, heads, seq, d), BF16),
        scratch_shapes=[pltpu.VMEM((tq, d), F32), pltpu.VMEM((tq, 1), F32)],
        compiler_params=_cparams("arbitrary", "arbitrary", "arbitrary"),
        name="stick_breaking_attention",
    )(q, kv, kv)


def _gelu_tanh(x):
    return 0.5 * x * (1.0 + jnp.tanh(0.7978845608028654 * (x + 0.044715 * (x * x * x))))


def _mix_ffn_kernel(*refs, even, final, f_chunks):
    if even:
        (h_ref, o_ref, y_ref, gw_ref, gb_ref, wtop_ref, wbot_ref, gtm_ref, nw_ref, sc_ref, sh_ref, gtf_ref,
         win_ref, wout_ref, fw_ref, out_ref, ybuf_ref) = refs
        y = _gelu_tanh(_from_chunk_major(y_ref, ybuf_ref))
        y = y * _sigmoid(_dot(y.astype(BF16), gw_ref[...]) + gb_ref[...])
        mix = _dot(o_ref[0], wtop_ref[...]) + _dot(y.astype(BF16), wbot_ref[...])
    else:
        (h_ref, o_ref, wtop_ref, gtm_ref, nw_ref, sc_ref, sh_ref, gtf_ref,
         win_ref, wout_ref, fw_ref, out_ref) = refs
        o = jnp.concatenate([o_ref[0, hh] for hh in range(o_ref.shape[1])], axis=1)
        mix = _dot(o, wtop_ref[...])
    h1 = h_ref[0] + gtm_ref[0] * mix
    hn = _norm_mod(h1, nw_ref[...], sc_ref[0], sh_ref[0]).astype(BF16)
    hidden = wout_ref.shape[0]
    fc = hidden // f_chunks
    acc = None
    for c in range(f_chunks):
        gate = _dot(hn, win_ref[:, c * fc:(c + 1) * fc])
        up = _dot(hn, win_ref[:, hidden + c * fc:hidden + (c + 1) * fc])
        part = _dot((_silu(gate) * up).astype(BF16), wout_ref[c * fc:(c + 1) * fc, :])
        acc = part if acc is None else acc + part
    h2 = h1 + gtf_ref[0] * acc
    if final:
        h2 = _rms(h2) * fw_ref[...]
    out_ref[0] = h2


def _mix_ffn(h, acts, mix_w, glu, gt_m, nw, sc, sh, gt_f, w_in, w_out, final_w, *, even, final):
    bsz, seq, d = h.shape
    tm = min(ROW_TILE, seq)
    hidden = w_out.shape[0]
    row = lambda b, t: (b, t, 0)
    per_b = lambda b, t: (b, 0, 0)
    vec_d = pl.BlockSpec((1, d), lambda b, t: (0, 0))
    mod_spec = pl.BlockSpec((1, 1, d), per_b)
    args, specs, scratch = [h], [pl.BlockSpec((1, tm, d), row)], []
    if even:
        o, y = acts
        half = o.shape[-1]
        nt = seq // tm
        glu_w, glu_b = glu
        args += [o, y, glu_w.astype(BF16), glu_b.astype(F32).reshape(1, -1),
                 mix_w[:half].astype(BF16), mix_w[half:].astype(BF16)]
        specs += [pl.BlockSpec((1, tm, half), row),
                  pl.BlockSpec((y.shape[0], tm // S5_CHUNK, y.shape[2]), lambda b, t: (0, b * nt + t, 0)),
                  _const_spec(glu_w.shape), pl.BlockSpec((1, glu_b.shape[-1]), lambda b, t: (0, 0)),
                  _const_spec((half, d)), _const_spec((d - half, d))]
        scratch = [pltpu.VMEM(((d - half) // LANES, tm, LANES), F32)]
    else:
        (o,) = acts
        args += [o, mix_w.astype(BF16)]
        specs += [pl.BlockSpec((1, o.shape[1], tm, o.shape[3]), lambda b, t: (b, 0, t, 0)), _const_spec((d, d))]
    args += [gt_m, nw.astype(F32).reshape(1, d), sc, sh, gt_f, w_in.astype(BF16), w_out.astype(BF16),
             final_w.astype(F32).reshape(1, d)]
    specs += [mod_spec, vec_d, mod_spec, mod_spec, mod_spec, _const_spec(w_in.shape), _const_spec(w_out.shape), vec_d]
    return pl.pallas_call(
        functools.partial(_mix_ffn_kernel, even=even, final=final, f_chunks=1),
        grid=(bsz, seq // tm),
        in_specs=specs,
        out_specs=pl.BlockSpec((1, tm, d), row),
        out_shape=jax.ShapeDtypeStruct((bsz, seq, d), F32),
        scratch_shapes=scratch,
        compiler_params=_cparams("arbitrary", "arbitrary"),
        name="outproj_swiglu_ffn",
    )(*args)


def kernel(x, c, ada_w, ada_b, norm_mix_w, norm_ffn_w, ffn_w_in, ffn_w_out, hy_w_in, hy_conv_w, hy_a_log, hy_dt_bias, hy_head_norm_w, s5_lam_re, s5_lam_im, s5_log_dt, s5_b_re, s5_b_im, s5_c_re, s5_c_im, s5_d, s5_glu_w, s5_glu_b, hy_w_out, sb_w_in, sb_w_out, final_norm_w):
    bsz, seq, d = x.shape
    depth = ada_w.shape[0]
    gdn_width = d // 2
    gdn_heads = gdn_width // GDN_HEAD_DIM
    s5_width = d - gdn_width

    mod = _modulation(c, ada_w, ada_b)
    h = x
    for i in range(depth):
        sh_m, sc_m, gt_m, sh_f, sc_f, gt_f = (mod[i, :bsz, k * d:(k + 1) * d].reshape(bsz, 1, d) for k in range(6))
        j = i // 2
        last = i == depth - 1
        if i % 2 == 0:
            w = hy_w_in[j]
            w4 = 4 * gdn_width
            gate_cols = jnp.pad(w[:, w4:w4 + 2 * gdn_heads], ((0, 0), (0, LANES - 2 * gdn_heads)))
            w_cat = jnp.concatenate([w[:, :w4], gate_cols, w[:, w4 + 2 * gdn_heads:]], axis=1).astype(BF16)
            qkv_pre, zg, ba, ut = _inproj(h, norm_mix_w[i], sc_m, sh_m, w_cat,
                                          (3 * gdn_width, gdn_width, LANES, s5_width), (F32, F32, F32, F32),
                                          chunk_major_last=True)
            o = _gdn(qkv_pre, zg, ba, hy_conv_w[j], hy_a_log[j], hy_dt_bias[j], hy_head_norm_w[j])
            y = _s5(ut, bsz, s5_lam_re[j], s5_lam_im[j], s5_log_dt[j], s5_b_re[j], s5_b_im[j],
                    s5_c_re[j], s5_c_im[j], s5_d[j])
            h = _mix_ffn(h, (o, y), hy_w_out[j], (s5_glu_w[j], s5_glu_b[j]), gt_m, norm_ffn_w[i], sc_f, sh_f, gt_f,
                         ffn_w_in[i], ffn_w_out[i], final_norm_w, even=True, final=last)
        else:
            q, kv = _inproj(h, norm_mix_w[i], sc_m, sh_m, sb_w_in[j].astype(BF16), (d, 2 * d), (BF16, BF16),
                            scales=(SB_HEAD_DIM ** -0.5 * LOG2_E, 1.0), head_dim=SB_HEAD_DIM)
            o = _sb_attention(q, kv)
            h = _mix_ffn(h, (o,), sb_w_out[j], None, gt_m, norm_ffn_w[i], sc_f, sh_f, gt_f,
                         ffn_w_in[i], ffn_w_out[i], final_norm_w, even=False, final=last)
    return h
```
